```python
import math
import jax, jax.numpy as jnp
from jax import lax
import numpy as np

D_MODEL = 1024
BATCH = 8
SEQ = 4096
DEPTH = 2
DEC_BATCH = 128
DEC_SEQ = 8
PAST_LEN = 16384
PAGE_SIZE = 128

N_MIXERS = 2
N_SSM_LAYERS = (DEPTH + 1) // 2
N_MLA_LAYERS = DEPTH // 2
D_FF = 2816
SSM_GROUP = 16
N_GROUPS = D_MODEL // SSM_GROUP
SSM_STATE = 64
DT_MIN = 1e-3
DT_MAX = 1e-1
N_HEADS = 16
QK_NOPE = 64
QK_ROPE = 32
V_HEAD = 64
KV_LORA = 256
Q_LORA = 768
ROPE_THETA = 10000.0
ATTN_SCALE = (QK_NOPE + QK_ROPE) ** -0.5
Q_BLOCK = 128
EPS = 1e-6

kernel_name = "hybrid_s5_mla_macaron_decode_step"

F32 = jnp.float32


def rms_norm(x, g):
    xf = x.astype(F32)
    y = xf * lax.rsqrt(jnp.mean(xf * xf, axis=-1, keepdims=True) + EPS)
    return (y * g.astype(F32)).astype(x.dtype)


def swiglu(x, w_gate, w_up, w_down):
    return (jax.nn.silu(x @ w_gate) * (x @ w_up)) @ w_down


def rope_tables(pos):
    inv = ROPE_THETA ** (-jnp.arange(0, QK_ROPE, 2, dtype=F32) / QK_ROPE)
    ang = pos.astype(F32)[:, None] * inv[None, :]
    return jnp.cos(ang), jnp.sin(ang)


def apply_rope(x, cos, sin):
    x1, x2 = jnp.split(x.astype(F32), 2, axis=-1)
    return jnp.concatenate([x1 * cos - x2 * sin, x1 * sin + x2 * cos], axis=-1).astype(x.dtype)


def s5_discretize(a_re, a_im, log_dt):
    dt = jnp.exp(log_dt.astype(F32))[:, None]
    lr = a_re.astype(F32)
    li = a_im.astype(F32)
    mag = jnp.exp(lr * dt)
    ang = li * dt
    ab_re = mag * jnp.cos(ang)
    ab_im = mag * jnp.sin(ang)
    nr = ab_re - 1.0
    den = lr * lr + li * li
    g_re = (nr * lr + ab_im * li) / den
    g_im = (ab_im * lr - nr * li) / den
    return ab_re, ab_im, g_re, g_im


def _ssm_combine(e1, e2):
    a1r, a1i, b1r, b1i = e1
    a2r, a2i, b2r, b2i = e2
    return (a2r * a1r - a2i * a1i,
            a2r * a1i + a2i * a1r,
            a2r * b1r - a2i * b1i + b2r,
            a2r * b1i + a2i * b1r + b2i)


def s5_mixer(u, h0_re, h0_im, a_re, a_im, log_dt, b_re, b_im, c_re, c_im, d, w_glu):
    B, T, _ = u.shape
    ab_re, ab_im, g_re, g_im = s5_discretize(a_re, a_im, log_dt)
    uf = u.astype(F32)
    ug = uf.reshape(B, T, N_GROUPS, SSM_GROUP)
    bu_re = jnp.einsum('btgi,gpi->btgp', ug, b_re.astype(F32))
    bu_im = jnp.einsum('btgi,gpi->btgp', ug, b_im.astype(F32))
    x_re = g_re * bu_re - g_im * bu_im
    x_im = g_re * bu_im + g_im * bu_re
    h0r = h0_re.astype(F32)
    h0i = h0_im.astype(F32)
    x_re = x_re.at[:, 0].add(ab_re * h0r - ab_im * h0i)
    x_im = x_im.at[:, 0].add(ab_re * h0i + ab_im * h0r)
    a_seq_re = jnp.broadcast_to(ab_re, (1, T, N_GROUPS, SSM_STATE))
    a_seq_im = jnp.broadcast_to(ab_im, (1, T, N_GROUPS, SSM_STATE))
    _, _, s_re, s_im = lax.associative_scan(_ssm_combine, (a_seq_re, a_seq_im, x_re, x_im), axis=1)
    y = (jnp.einsum('gip,btgp->btgi', c_re.astype(F32), s_re)
         - jnp.einsum('gip,btgp->btgi', c_im.astype(F32), s_im))
    y = y.reshape(B, T, D_MODEL) + d.astype(F32) * uf
    h = jax.nn.gelu(y).astype(u.dtype)
    z = h @ w_glu
    out = z[..., :D_MODEL] * jax.nn.sigmoid(z[..., D_MODEL:])
    return out, s_re[:, -1], s_im[:, -1]


def mla_project(x, cos, sin, w_in, q_norm, kv_norm, w_uq):
    B, T, _ = x.shape
    h = x @ w_in
    c_q = rms_norm(h[..., :Q_LORA], q_norm)
    c_kv = rms_norm(h[..., Q_LORA:Q_LORA + KV_LORA], kv_norm)
    k_pe = apply_rope(h[..., Q_LORA + KV_LORA:], cos, sin)
    q = (c_q @ w_uq).reshape(B, T, N_HEADS, QK_NOPE + QK_ROPE)
    q_nope = q[..., :QK_NOPE]
    q_pe = apply_rope(q[..., QK_NOPE:], cos[:, None, :], sin[:, None, :])
    return q_nope, q_pe, c_kv, k_pe


def prompt_attention(q_nope, q_pe, c_kv, k_pe, w_ukv):
    B, S, _ = c_kv.shape
    kv = (c_kv @ w_ukv).reshape(B, S, N_HEADS, QK_NOPE + V_HEAD)
    k_nope = kv[..., :QK_NOPE]
    v = kv[..., QK_NOPE:]
    n_blk = S // Q_BLOCK
    qn = q_nope.reshape(B, n_blk, Q_BLOCK, N_HEADS, QK_NOPE).swapaxes(0, 1)
    qp = q_pe.reshape(B, n_blk, Q_BLOCK, N_HEADS, QK_ROPE).swapaxes(0, 1)
    k_pos = jnp.arange(S)

    def block(args):
        i, qn_b, qp_b = args
        s = (jnp.einsum('bqhd,bkhd->bhqk', qn_b, k_nope, preferred_element_type=F32)
             + jnp.einsum('bqhr,bkr->bhqk', qp_b, k_pe, preferred_element_type=F32)) * ATTN_SCALE
        q_pos = i * Q_BLOCK + jnp.arange(Q_BLOCK)
        s = jnp.where(k_pos[None, :] <= q_pos[:, None], s, -jnp.inf)
        p = jax.nn.softmax(s, axis=-1).astype(v.dtype)
        return jnp.einsum('bhqk,bkhd->bqhd', p, v)

    out = lax.map(block, (jnp.arange(n_blk), qn, qp))
    return out.swapaxes(0, 1).reshape(B, S, N_HEADS, V_HEAD)


def sample_attention(q_nope, q_pe, c_kv, k_pe, w_ukv, cache_ckv, cache_kpe, page_table):
    w = w_ukv.reshape(KV_LORA, N_HEADS, QK_NOPE + V_HEAD).astype(F32)
    w_uk = w[..., :QK_NOPE]
    w_uv = w[..., QK_NOPE:]
    T = c_kv.shape[1]
    q_lat = jnp.einsum('bthd,chd->bhtc', q_nope.astype(F32), w_uk)
    q_r = q_pe.astype(F32).transpose(0, 2, 1, 3)

    def scores(ckv, kpe):
        return (jnp.einsum('bhtc,bkc->bhtk', q_lat, ckv.astype(F32))
                + jnp.einsum('bhtr,bkr->bhtk', q_r, kpe.astype(F32))) * ATTN_SCALE

    s = scores(c_kv, k_pe)
    causal = jnp.tril(jnp.ones((T, T), dtype=bool))
    s = jnp.where(causal, s, -jnp.inf)
    m = jnp.max(s, axis=-1)
    p = jnp.exp(s - m[..., None])
    l = jnp.sum(p, axis=-1)
    acc = jnp.einsum('bhtk,bkc->bhtc', p, c_kv.astype(F32))

    def page_step(carry, phys):
        m, l, acc = carry
        ckv = cache_ckv[phys].astype(F32)
        kpe = cache_kpe[phys]
        s = scores(ckv, kpe)
        m_new = jnp.maximum(m, jnp.max(s, axis=-1))
        corr = jnp.exp(m - m_new)
        p = jnp.exp(s - m_new[..., None])
        l = l * corr + jnp.sum(p, axis=-1)
        acc = acc * corr[..., None] + jnp.einsum('bhtk,bkc->bhtc', p, ckv)
        return (m_new, l, acc), None

    (m, l, acc), _ = lax.scan(page_step, (m, l, acc), page_table.T)
    o_lat = acc / l[..., None]
    return jnp.einsum('bhtc,chv->bthv', o_lat, w_uv).astype(q_nope.dtype)


def forward(x, pos, h0_re, h0_im, attend, p):
    B, T, _ = x.shape
    cos, sin = rope_tables(pos)
    new_re, new_im, new_lat, new_kpe = [], [], [], []
    for i in range(DEPTH):
        j = i // N_MIXERS
        h = swiglu(rms_norm(x, p['norm_pre'][i, 0]), p['ffn_w_gate'][i, 0], p['ffn_w_up'][i, 0], p['ffn_w_down'][i, 0])
        x = x + 0.5 * rms_norm(h, p['norm_post'][i, 0])
        h = rms_norm(x, p['norm_pre'][i, 1])
        if i % N_MIXERS == 0:
            h, s_re, s_im = s5_mixer(h, h0_re[j], h0_im[j], p['ssm_a_re'][j], p['ssm_a_im'][j], p['ssm_log_dt'][j],
                                     p['ssm_b_re'][j], p['ssm_b_im'][j], p['ssm_c_re'][j], p['ssm_c_im'][j],
                                     p['ssm_d'][j], p['ssm_w_glu'][j])
            new_re.append(s_re)
            new_im.append(s_im)
        else:
            q_nope, q_pe, c_kv, k_pe = mla_project(h, cos, sin, p['mla_w_in'][j], p['mla_q_norm'][j],
                                                   p['mla_kv_norm'][j], p['mla_w_uq'][j])
            o = attend(j, q_nope, q_pe, c_kv, k_pe, p['mla_w_ukv'][j])
            h = o.reshape(B, T, N_HEADS * V_HEAD) @ p['mla_w_o'][j]
            new_lat.append(c_kv)
            new_kpe.append(k_pe)
        x = x + rms_norm(h, p['norm_post'][i, 1])
        h = swiglu(rms_norm(x, p['norm_pre'][i, 2]), p['ffn_w_gate'][i, 1], p['ffn_w_up'][i, 1], p['ffn_w_down'][i, 1])
        x = x + 0.5 * rms_norm(h, p['norm_post'][i, 2])
    return x, jnp.stack(new_re), jnp.stack(new_im), jnp.stack(new_lat), jnp.stack(new_kpe)


def setup_inputs(seed: int = 0) -> dict:
    key = jax.random.key(seed)
    ks = jax.random.split(key, 32)
    n_pages = PAST_LEN // PAGE_SIZE
    n_phys = (DEC_BATCH * n_pages * 5) // 4
    nrm = lambda k, shape, s: jax.random.normal(k, shape, F32) * s
    perm = jax.random.permutation(ks[0], n_phys)
    page_table = perm[:DEC_BATCH * n_pages].reshape(DEC_BATCH, n_pages).astype(jnp.int32)
    a_im0 = math.pi * jnp.arange(SSM_STATE, dtype=F32)
    return {
        'x_prompt': nrm(ks[1], (BATCH, SEQ, D_MODEL), 1.0),
        'x_sample': nrm(ks[2], (DEC_BATCH, DEC_SEQ, D_MODEL), 1.0),
        'state_ssm_re': nrm(ks[3], (N_SSM_LAYERS, DEC_BATCH, N_GROUPS, SSM_STATE), 0.1),
        'state_ssm_im': nrm(ks[4], (N_SSM_LAYERS, DEC_BATCH, N_GROUPS, SSM_STATE), 0.1),
        'cache_kv_latent': nrm(ks[5], (N_MLA_LAYERS, n_phys, PAGE_SIZE, KV_LORA), 1.0),
        'cache_k_rope': nrm(ks[6], (N_MLA_LAYERS, n_phys, PAGE_SIZE, QK_ROPE), 1.0),
        'page_table': page_table,
        'norm_pre': 1.0 + nrm(ks[7], (DEPTH, 3, D_MODEL), 0.01),
        'norm_post': 1.0 + nrm(ks[8], (DEPTH, 3, D_MODEL), 0.01),
        'ffn_w_gate': nrm(ks[9], (DEPTH, 2, D_MODEL, D_FF), D_MODEL ** -0.5),
        'ffn_w_up': nrm(ks[10], (DEPTH, 2, D_MODEL, D_FF), D_MODEL ** -0.5),
        'ffn_w_down': nrm(ks[11], (DEPTH, 2, D_FF, D_MODEL), D_FF ** -0.5),
        'ssm_a_re': -0.5 + nrm(ks[12], (N_SSM_LAYERS, N_GROUPS, SSM_STATE), 0.01),
        'ssm_a_im': a_im0 + nrm(ks[13], (N_SSM_LAYERS, N_GROUPS, SSM_STATE), 0.01),
        'ssm_log_dt': jax.random.uniform(ks[14], (N_SSM_LAYERS, N_GROUPS), F32, math.log(DT_MIN), math.log(DT_MAX)),
        'ssm_b_re': nrm(ks[15], (N_SSM_LAYERS, N_GROUPS, SSM_STATE, SSM_GROUP), (2 * SSM_GROUP) ** -0.5),
        'ssm_b_im': nrm(ks[16], (N_SSM_LAYERS, N_GROUPS, SSM_STATE, SSM_GROUP), (2 * SSM_GROUP) ** -0.5),
        'ssm_c_re': nrm(ks[17], (N_SSM_LAYERS, N_GROUPS, SSM_GROUP, SSM_STATE), (2 * SSM_STATE) ** -0.5),
        'ssm_c_im': nrm(ks[18], (N_SSM_LAYERS, N_GROUPS, SSM_GROUP, SSM_STATE), (2 * SSM_STATE) ** -0.5),
        'ssm_d': nrm(ks[19], (N_SSM_LAYERS, D_MODEL), 1.0),
        'ssm_w_glu': nrm(ks[20], (N_SSM_LAYERS, D_MODEL, 2 * D_MODEL), D_MODEL ** -0.5),
        'mla_w_in': nrm(ks[21], (N_MLA_LAYERS, D_MODEL, Q_LORA + KV_LORA + QK_ROPE), D_MODEL ** -0.5),
        'mla_q_norm': 1.0 + nrm(ks[22], (N_MLA_LAYERS, Q_LORA), 0.01),
        'mla_kv_norm': 1.0 + nrm(ks[23], (N_MLA_LAYERS, KV_LORA), 0.01),
        'mla_w_uq': nrm(ks[24], (N_MLA_LAYERS, Q_LORA, N_HEADS * (QK_NOPE + QK_ROPE)), Q_LORA ** -0.5),
        'mla_w_ukv': nrm(ks[25], (N_MLA_LAYERS, KV_LORA, N_HEADS * (QK_NOPE + V_HEAD)), KV_LORA ** -0.5),
        'mla_w_o': nrm(ks[26], (N_MLA_LAYERS, N_HEADS * V_HEAD, D_MODEL), (N_HEADS * V_HEAD) ** -0.5),
    }


def reference(x_prompt, x_sample, state_ssm_re, state_ssm_im, cache_kv_latent, cache_k_rope, page_table,
              norm_pre, norm_post, ffn_w_gate, ffn_w_up, ffn_w_down,
              ssm_a_re, ssm_a_im, ssm_log_dt, ssm_b_re, ssm_b_im, ssm_c_re, ssm_c_im, ssm_d, ssm_w_glu,
              mla_w_in, mla_q_norm, mla_kv_norm, mla_w_uq, mla_w_ukv, mla_w_o):
    p = {
        'norm_pre': norm_pre, 'norm_post': norm_post,
        'ffn_w_gate': ffn_w_gate, 'ffn_w_up': ffn_w_up, 'ffn_w_down': ffn_w_down,
        'ssm_a_re': ssm_a_re, 'ssm_a_im': ssm_a_im, 'ssm_log_dt': ssm_log_dt,
        'ssm_b_re': ssm_b_re, 'ssm_b_im': ssm_b_im, 'ssm_c_re': ssm_c_re, 'ssm_c_im': ssm_c_im,
        'ssm_d': ssm_d, 'ssm_w_glu': ssm_w_glu,
        'mla_w_in': mla_w_in, 'mla_q_norm': mla_q_norm, 'mla_kv_norm': mla_kv_norm,
        'mla_w_uq': mla_w_uq, 'mla_w_ukv': mla_w_ukv, 'mla_w_o': mla_w_o,
    }
    b_p, s_p = x_prompt.shape[0], x_prompt.shape[1]
    h0 = jnp.zeros((N_SSM_LAYERS, b_p, N_GROUPS, SSM_STATE), dtype=state_ssm_re.dtype)
    pos_p = jnp.arange(s_p, dtype=jnp.int32)
    prompt_attend = lambda j, qn, qp, ckv, kp, w: prompt_attention(qn, qp, ckv, kp, w)
    y_prompt, ssm_re_p, ssm_im_p, lat_p, kpe_p = forward(x_prompt, pos_p, h0, h0, prompt_attend, p)
    pos_s = PAST_LEN + jnp.arange(x_sample.shape[1], dtype=jnp.int32)
    sample_attend = lambda j, qn, qp, ckv, kp, w: sample_attention(qn, qp, ckv, kp, w, cache_kv_latent[j],
                                                                  cache_k_rope[j], page_table)
    y_sample, ssm_re_s, ssm_im_s, lat_s, kpe_s = forward(x_sample, pos_s, state_ssm_re, state_ssm_im,
                                                         sample_attend, p)
    return (y_prompt, y_sample, ssm_re_p, ssm_im_p, ssm_re_s, ssm_im_s, lat_p, kpe_p, lat_s, kpe_s)
```

```python
import functools
import math
from typing import NamedTuple

import jax
import jax.numpy as jnp
from jax import lax
from jax.experimental import pallas as pl
from jax.experimental.pallas import tpu as pltpu

F32 = jnp.float32
BF16 = jnp.bfloat16

LANES = 128
SUBLANES = 8
MXU_DIM = 256
VMEM_LIMIT_BYTES = 56 * 1024 * 1024

EPS = 1e-6
ROPE_THETA = 10000.0


class Dims(NamedTuple):
    d_model: int = 1024
    d_ff: int = 2816
    ssm_group: int = 16
    ssm_state: int = 64
    n_heads: int = 16
    qk_nope: int = 64
    qk_rope: int = 32
    v_head: int = 64
    kv_lora: int = 256
    q_lora: int = 768
    page_size: int = 128
    tm: int = 512
    ffn_chunk: int = 1408
    s5_rows: int = 256
    tq: int = 512
    pages_per_step: int = 16
    out_bb: int = 32


DIMS = Dims()


def _rms(x, g):
    ms = jnp.mean(x * x, axis=-1, keepdims=True)
    return x * lax.rsqrt(ms + EPS) * g


def _const_spec(shape):
    zeros = (0,) * len(shape)
    return pl.BlockSpec(shape, lambda *_: zeros, pipeline_mode=pl.Buffered(1))


def _params(*sem):
    return pltpu.CompilerParams(dimension_semantics=sem, vmem_limit_bytes=VMEM_LIMIT_BYTES)


def _ffn_body(x_ref, gpre_ref, gpost_ref, wg_ref, wu_ref, wd_ref, o_ref, *, d_ff, chunk):
    x = x_ref[...]
    h = _rms(x, gpre_ref[...]).astype(BF16)
    y = None
    for c in range(d_ff // chunk):
        sl = slice(c * chunk, (c + 1) * chunk)
        g = jnp.dot(h, wg_ref[:, sl], preferred_element_type=F32)
        u = jnp.dot(h, wu_ref[:, sl], preferred_element_type=F32)
        a = (g * jax.nn.sigmoid(g) * u).astype(BF16)
        part = jnp.dot(a, wd_ref[sl, :], preferred_element_type=F32)
        y = part if y is None else y + part
    o_ref[...] = x + 0.5 * _rms(y, gpost_ref[...])


def _ffn(x, gpre, gpost, wg, wu, wd, dm):
    n, d = x.shape
    tm = min(dm.tm, n)
    body = functools.partial(_ffn_body, d_ff=dm.d_ff, chunk=dm.ffn_chunk)
    return pl.pallas_call(
        body,
        grid=(n // tm,),
        in_specs=[pl.BlockSpec((tm, d), lambda i: (i, 0)),
                  _const_spec((1, d)), _const_spec((1, d)),
                  _const_spec(wg.shape), _const_spec(wu.shape), _const_spec(wd.shape)],
        out_specs=pl.BlockSpec((tm, d), lambda i: (i, 0)),
        out_shape=jax.ShapeDtypeStruct((n, d), F32),
        compiler_params=_params("arbitrary"),
        name="ffn",
    )(x, gpre, gpost, wg, wu, wd)


def _ssm_prep_body(are_ref, aim_ref, ldt_ref, bre_ref, bim_ref, abre_ref, abim_ref, bbre_ref, bbim_ref):
    dt = jnp.exp(ldt_ref[...])
    lr = are_ref[...]
    li = aim_ref[...]
    mag = jnp.exp(lr * dt)
    ang = li * dt
    ab_re = mag * jnp.cos(ang)
    ab_im = mag * jnp.sin(ang)
    nr = ab_re - 1.0
    den = lr * lr + li * li
    g_re = (nr * lr + ab_im * li) / den
    g_im = (ab_im * lr - nr * li) / den
    abre_ref[...] = ab_re
    abim_ref[...] = ab_im
    b_re = bre_ref[...]
    b_im = bim_ref[...]
    bbre_ref[...] = g_re * b_re - g_im * b_im
    bbim_ref[...] = g_re * b_im + g_im * b_re


def _ssm_prep(a_re, a_im, log_dt, b_re, b_im):
    g, p = a_re.shape
    i = b_re.shape[-1]
    n = g * p
    col = lambda a: a.reshape(n, 1)
    ldt = jnp.broadcast_to(log_dt[:, None], (g, p))
    shp = [jax.ShapeDtypeStruct((n, 1), F32)] * 2 + [jax.ShapeDtypeStruct((n, i), F32)] * 2
    ab_re, ab_im, bb_re, bb_im = pl.pallas_call(_ssm_prep_body, out_shape=shp, name="ssm_prep")(
        col(a_re), col(a_im), col(ldt), b_re.reshape(n, i), b_im.reshape(n, i))
    return ab_re.reshape(g, p), ab_im.reshape(g, p), bb_re.reshape(g, p, i), bb_im.reshape(g, p, i)


def _s5_body(x_ref, h0_ref, gpre_ref, gpost_ref, bblk_ref, cw_ref, are_ref, aim_ref, d_ref, wglu_ref,
             o_ref, hfin_ref, s_ref, hst_ref, *, tc, bb, n_ktile, slabs_per_ktile, slabs_per_out):
    ti = pl.program_id(1)
    rows = tc * bb
    d = x_ref.shape[-1]
    n_slab = s_ref.shape[0]
    x = x_ref[...].reshape(rows, d)
    u = _rms(x, gpre_ref[...])
    ub = u.astype(BF16)
    for j in range(n_ktile):
        bu = jnp.dot(ub[:, j * MXU_DIM:(j + 1) * MXU_DIM], bblk_ref[j], preferred_element_type=F32)
        for q in range(slabs_per_ktile):
            s_ref[j * slabs_per_ktile + q] = bu[:, q * 2 * LANES:(q + 1) * 2 * LANES]

    @pl.when(ti == 0)
    def _():
        hst_ref[...] = h0_ref[...]

    n_par = 4
    n_sub = bb // SUBLANES

    def slab_group(kg, carry):
        ks = [kg * n_par + q for q in range(n_par)]
        ar = [jnp.broadcast_to(are_ref[k], (SUBLANES, LANES)) for k in ks]
        ai = [jnp.broadcast_to(aim_ref[k], (SUBLANES, LANES)) for k in ks]

        def sub_group(sg, carry2):
            r0 = pl.multiple_of(sg * SUBLANES, SUBLANES)
            hr = tuple(hst_ref[k, pl.ds(r0, SUBLANES), 0:LANES] for k in ks)
            hi = tuple(hst_ref[k, pl.ds(r0, SUBLANES), LANES:2 * LANES] for k in ks)

            def step(t, h):
                hr, hi = h
                r = pl.multiple_of(t * bb + r0, SUBLANES)
                nr, ni = [], []
                for q, k in enumerate(ks):
                    xr = s_ref[k, pl.ds(r, SUBLANES), 0:LANES]
                    xi = s_ref[k, pl.ds(r, SUBLANES), LANES:2 * LANES]
                    vr = ar[q] * hr[q] - ai[q] * hi[q] + xr
                    vi = ar[q] * hi[q] + ai[q] * hr[q] + xi
                    s_ref[k, pl.ds(r, SUBLANES), 0:LANES] = vr
                    s_ref[k, pl.ds(r, SUBLANES), LANES:2 * LANES] = vi
                    nr.append(vr)
                    ni.append(vi)
                return tuple(nr), tuple(ni)

            hr, hi = lax.fori_loop(0, tc, step, (hr, hi), unroll=min(tc, 8))
            for q, k in enumerate(ks):
                hst_ref[k, pl.ds(r0, SUBLANES), 0:LANES] = hr[q]
                hst_ref[k, pl.ds(r0, SUBLANES), LANES:2 * LANES] = hi[q]
            return carry2

        lax.fori_loop(0, n_sub, sub_group, 0)
        return carry

    lax.fori_loop(0, n_slab // n_par, slab_group, 0)
    hfin_ref[...] = hst_ref[...]

    pieces = []
    for j in range(n_slab // slabs_per_out):
        acc = None
        for q in range(slabs_per_out):
            k = j * slabs_per_out + q
            part = jnp.dot(s_ref[k].astype(BF16), cw_ref[k], preferred_element_type=F32)
            acc = part if acc is None else acc + part
        pieces.append(acc)
    y = jnp.concatenate(pieces, axis=-1) + d_ref[...] * u
    hg = jax.nn.gelu(y).astype(BF16)
    z = jnp.dot(hg, wglu_ref[...], preferred_element_type=F32)
    mix = z[:, :d] * jax.nn.sigmoid(z[:, d:])
    o_ref[...] = (x + _rms(mix, gpost_ref[...])).reshape(tc, bb, d)


def _s5(xt, h0, gpre, gpost, bblk, cw, a_re, a_im, dvec, wglu, dm, *, tc, bb):
    t, b, d = xt.shape
    n_slab = cw.shape[0]
    n_ktile = bblk.shape[0]
    out_lanes_per_slab = (LANES // dm.ssm_state) * dm.ssm_group
    body = functools.partial(_s5_body, tc=tc, bb=bb, n_ktile=n_ktile, slabs_per_ktile=n_slab // n_ktile,
                             slabs_per_out=LANES // out_lanes_per_slab)
    return pl.pallas_call(
        body,
        grid=(b // bb, t // tc),
        in_specs=[pl.BlockSpec((tc, bb, d), lambda bi, ti: (ti, bi, 0)),
                  pl.BlockSpec((n_slab, bb, 2 * LANES), lambda bi, ti: (0, bi, 0)),
                  _const_spec((1, d)), _const_spec((1, d)),
                  _const_spec(bblk.shape), _const_spec(cw.shape),
                  _const_spec(a_re.shape), _const_spec(a_im.shape),
                  _const_spec((1, d)), _const_spec(wglu.shape)],
        out_specs=[pl.BlockSpec((tc, bb, d), lambda bi, ti: (ti, bi, 0)),
                   pl.BlockSpec((n_slab, bb, 2 * LANES), lambda bi, ti: (0, bi, 0))],
        out_shape=[jax.ShapeDtypeStruct((t, b, d), F32),
                   jax.ShapeDtypeStruct((n_slab, b, 2 * LANES), F32)],
        scratch_shapes=[pltpu.VMEM((n_slab, tc * bb, 2 * LANES), F32),
                        pltpu.VMEM((n_slab, bb, 2 * LANES), F32)],
        compiler_params=_params("arbitrary", "arbitrary"),
        name="s5_mixer",
    )(xt, h0, gpre, gpost, bblk, cw, a_re, a_im, dvec, wglu)


def _s5_weights(bb_re, bb_im, c_re, c_im, ab_re, ab_im, dm):
    g, p, i = bb_re.shape
    gps = LANES // p
    n_slab = g // gps
    g_tile = MXU_DIM // i
    n_ktile = g // g_tile
    eye_t = jnp.eye(g_tile, dtype=F32)

    def b_tiles(bm):
        bt = bm.transpose(0, 2, 1).reshape(n_ktile, g_tile, i, p)
        return jnp.einsum('jgip,gh->jgihp', bt, eye_t).reshape(n_ktile, g_tile * i, g_tile // gps, LANES)

    bblk = jnp.stack([b_tiles(bb_re), b_tiles(bb_im)], axis=3)
    bblk = bblk.reshape(n_ktile, g_tile * i, g_tile * p * 2).astype(BF16)

    spo = LANES // (gps * i)
    eye_g = jnp.eye(gps, dtype=F32)
    sel = jax.nn.one_hot(jnp.arange(n_slab) % spo, spo, dtype=F32)

    def c_slabs(cm):
        ct = cm.reshape(n_slab, gps, i, p)
        base = jnp.einsum('khip,hg->khpgi', ct, eye_g).reshape(n_slab, gps * p, gps * i)
        return jnp.einsum('krc,kq->krqc', base, sel).reshape(n_slab, LANES, LANES)

    cw = jnp.concatenate([c_slabs(c_re), -c_slabs(c_im)], axis=1).astype(BF16)
    a_re = ab_re.reshape(n_slab, 1, LANES)
    a_im = ab_im.reshape(n_slab, 1, LANES)
    return bblk, cw, a_re, a_im


def _state_to_slabs(h_re, h_im):
    b = h_re.shape[0]
    re = h_re.reshape(b, -1, LANES)
    im = h_im.reshape(b, -1, LANES)
    return jnp.concatenate([re, im], axis=-1).transpose(1, 0, 2)


def _slabs_to_state(hs, g, p):
    n_slab, b, _ = hs.shape
    hs = hs.transpose(1, 0, 2)
    re = hs[:, :, :LANES].reshape(b, g, p)
    im = hs[:, :, LANES:].reshape(b, g, p)
    return re, im


def _rope_block(blk, t1, t2):
    return blk * t1 + pltpu.roll(blk, 3 * LANES // 4, 1) * t2


def _mla_proj_body(x_ref, gpre_ref, win_ref, qn_ref, kvn_ref, wuq_ref, wuk_ref, wuv_ref, t1_ref, t2_ref,
                   q_ref, ckv_ref, kpe_ref, *kv_refs, q_lora, kv_lora, n_heads, qk_rope):
    x = x_ref[...]
    h = _rms(x, gpre_ref[...]).astype(BF16)
    hin = jnp.dot(h, win_ref[...], preferred_element_type=F32)
    cq = _rms(hin[:, :q_lora], qn_ref[...]).astype(BF16)
    ckv = _rms(hin[:, q_lora:q_lora + kv_lora], kvn_ref[...])
    t1 = t1_ref[...]
    t2 = t2_ref[...]
    kpe_blk = _rope_block(hin[:, q_lora + kv_lora:], t1, t2)
    ckv_ref[...] = ckv
    kpe_ref[...] = pltpu.roll(kpe_blk, LANES // 2, 1)[:, :qk_rope]
    q = jnp.dot(cq, wuq_ref[...], preferred_element_type=F32)
    for hd in range(n_heads):
        sl = slice(hd * LANES, (hd + 1) * LANES)
        q_ref[:, sl] = _rope_block(q[:, sl], t1, t2).astype(BF16)
    if kv_refs:
        k_ref, v_ref = kv_refs
        cb = ckv.astype(BF16)
        kn = jnp.dot(cb, wuk_ref[...], preferred_element_type=F32)
        for hd in range(n_heads):
            sl = slice(hd * LANES, (hd + 1) * LANES)
            k_ref[:, sl] = (kn[:, sl] + kpe_blk).astype(BF16)
        v_ref[...] = jnp.dot(cb, wuv_ref[...], preferred_element_type=F32).astype(BF16)


def _mla_proj(x, gpre, win, qn, kvn, wuq, wuk, wuv, t1, t2, dm, *, with_kv):
    n, d = x.shape
    tm = min(dm.tm, n)
    n_tab = t1.shape[0] // tm
    hq = dm.n_heads * LANES
    hv = dm.n_heads * dm.v_head
    body = functools.partial(_mla_proj_body, q_lora=dm.q_lora, kv_lora=dm.kv_lora, n_heads=dm.n_heads,
                             qk_rope=dm.qk_rope)
    row = lambda w: pl.BlockSpec((tm, w), lambda i: (i, 0))
    tab = pl.BlockSpec((tm, LANES), lambda i: (i % n_tab, 0))
    out_specs = [row(hq), row(dm.kv_lora), row(dm.qk_rope)]
    out_shape = [jax.ShapeDtypeStruct((n, hq), BF16), jax.ShapeDtypeStruct((n, dm.kv_lora), F32),
                 jax.ShapeDtypeStruct((n, dm.qk_rope), F32)]
    if with_kv:
        out_specs += [row(hq), row(hv)]
        out_shape += [jax.ShapeDtypeStruct((n, hq), BF16), jax.ShapeDtypeStruct((n, hv), BF16)]
    return pl.pallas_call(
        body,
        grid=(n // tm,),
        in_specs=[row(d), _const_spec((1, d)), _const_spec(win.shape), _const_spec(qn.shape),
                  _const_spec(kvn.shape), _const_spec(wuq.shape), _const_spec(wuk.shape),
                  _const_spec(wuv.shape), tab, tab],
        out_specs=out_specs,
        out_shape=out_shape,
        compiler_params=_params("arbitrary"),
        name="mla_proj_kv" if with_kv else "mla_proj",
    )(x, gpre, win, qn, kvn, wuq, wuk, wuv, t1, t2)


def _rope_tables(pos, dm):
    half = dm.qk_rope // 2
    inv = ROPE_THETA ** (-jnp.arange(0, dm.qk_rope, 2, dtype=F32) / dm.qk_rope)
    ang = pos.astype(F32)[:, None] * inv[None, :]
    cos, sin = jnp.cos(ang), jnp.sin(ang)
    n = pos.shape[0]
    pad = LANES - dm.qk_nope - dm.qk_rope
    t1 = jnp.concatenate([jnp.ones((n, dm.qk_nope), F32), cos, cos, jnp.zeros((n, pad), F32)], axis=1)
    t2 = jnp.concatenate([jnp.zeros((n, dm.qk_nope), F32), -sin, sin, jnp.zeros((n, pad), F32)], axis=1)
    del half
    return t1, t2


def _mla_weights(w_in, w_uq, w_ukv, w_o, dm):
    hn, nope, rope, vh = dm.n_heads, dm.qk_nope, dm.qk_rope, dm.v_head
    half = rope // 2
    d = w_in.shape[0]
    lat = dm.q_lora + dm.kv_lora
    kpe = w_in[:, lat:]
    kpe_sw = jnp.concatenate([kpe[:, half:], kpe[:, :half]], axis=1)
    win = jnp.concatenate([w_in[:, :lat], jnp.zeros((d, nope), F32), kpe, kpe_sw], axis=1).astype(BF16)
    uq = w_uq.reshape(dm.q_lora, hn, nope + rope)
    wuq = jnp.concatenate([uq, uq[:, :, nope + half:], uq[:, :, nope:nope + half]], axis=2)
    wuq = wuq.reshape(dm.q_lora, hn * LANES).astype(BF16)
    ukv = w_ukv.reshape(dm.kv_lora, hn, nope + vh)
    wuk = jnp.concatenate([ukv[:, :, :nope], jnp.zeros((dm.kv_lora, hn, LANES - nope), F32)], axis=2)
    wuk = wuk.reshape(dm.kv_lora, hn * LANES).astype(BF16)
    wuv = ukv[:, :, nope:].reshape(dm.kv_lora, hn * vh).astype(BF16)
    wuk_t = jnp.concatenate([ukv[:, :, :nope].transpose(1, 2, 0),
                             jnp.zeros((hn, LANES - nope, dm.kv_lora), F32)], axis=1).astype(BF16)
    wuv_p = jnp.concatenate([ukv[:, :, nope:].transpose(1, 0, 2),
                             jnp.zeros((hn, dm.kv_lora, LANES - vh), F32)], axis=2).astype(BF16)
    wo_p = jnp.concatenate([w_o.reshape(hn, vh, d), jnp.zeros((hn, LANES - vh, d), F32)], axis=1).astype(BF16)
    return dict(win=win, wuq=wuq, wuk=wuk, wuv=wuv, wuk_t=wuk_t, wuv_p=wuv_p, wo_p=wo_p, wo=w_o.astype(BF16))


def _attn_prompt_body(q_ref, k_ref, v_ref, o_ref, *, tq, scale, v_head):
    qi = pl.program_id(2)
    row = lax.broadcasted_iota(jnp.int32, (tq, tq), 0)
    col = lax.broadcasted_iota(jnp.int32, (tq, tq), 1)
    causal = col <= row
    outs = []
    for hh in range(q_ref.shape[-1] // LANES):
        qh = q_ref[:, hh * LANES:(hh + 1) * LANES]

        def block(ki, carry, masked, hh=hh, qh=qh):
            m, l, acc = carry
            r = pl.multiple_of(ki * tq, tq)
            kk = k_ref[pl.ds(r, tq), hh * LANES:(hh + 1) * LANES]
            vv = v_ref[pl.ds(r, tq), :]
            s = lax.dot_general(qh, kk, (((1,), (1,)), ((), ())), preferred_element_type=F32) * scale
            if masked:
                s = jnp.where(causal, s, -jnp.inf)
            m_new = jnp.maximum(m, jnp.max(s, axis=-1, keepdims=True))
            alpha = jnp.exp(m - m_new)
            p = jnp.exp(s - m_new)
            l = alpha * l + jnp.sum(p, axis=-1, keepdims=True)
            acc = alpha * acc + jnp.dot(p.astype(BF16), vv, preferred_element_type=F32)
            return m_new, l, acc

        init = (jnp.full((tq, 1), -jnp.inf, F32), jnp.zeros((tq, 1), F32), jnp.zeros((tq, v_ref.shape[-1]), F32))
        carry = lax.fori_loop(0, qi, functools.partial(block, masked=False), init)
        m, l, acc = block(qi, carry, True)
        outs.append(acc / l)
    lane = lax.broadcasted_iota(jnp.int32, outs[0].shape, 1)
    o = outs[0]
    for hh in range(1, len(outs)):
        o = jnp.where(lane >= hh * v_head, outs[hh], o)
    o_ref[...] = o.astype(o_ref.dtype)


def _attn_prompt(q, k, v, batch, seq, dm):
    tq = min(dm.tq, seq)
    nq = seq // tq
    hp = dm.n_heads // 2
    scale = (dm.qk_nope + dm.qk_rope) ** -0.5
    k3 = k.reshape(batch, seq, -1)
    v3 = v.reshape(batch, seq, -1)
    body = functools.partial(_attn_prompt_body, tq=tq, scale=scale, v_head=dm.v_head)
    return pl.pallas_call(
        body,
        grid=(batch, hp, nq),
        in_specs=[pl.BlockSpec((tq, 2 * LANES), lambda b, h, i: (b * nq + i, h)),
                  pl.BlockSpec((None, seq, 2 * LANES), lambda b, h, i: (b, 0, h)),
                  pl.BlockSpec((None, seq, LANES), lambda b, h, i: (b, 0, h))],
        out_specs=pl.BlockSpec((tq, LANES), lambda b, h, i: (b * nq + i, h)),
        out_shape=jax.ShapeDtypeStruct((batch * seq, dm.n_heads * dm.v_head), BF16),
        compiler_params=_params("arbitrary", "arbitrary", "arbitrary"),
        name="attn_prompt",
    )(q, k3, v3)


def _attn_out_body(x_ref, o_ref, wo_ref, gpost_ref, y_ref):
    h = jnp.dot(o_ref[...], wo_ref[...], preferred_element_type=F32)
    y_ref[...] = x_ref[...] + _rms(h, gpost_ref[...])


def _attn_out(x, o, wo, gpost, dm):
    n, d = x.shape
    tm = min(dm.tm, n)
    return pl.pallas_call(
        _attn_out_body,
        grid=(n // tm,),
        in_specs=[pl.BlockSpec((tm, d), lambda i: (i, 0)), pl.BlockSpec((tm, o.shape[1]), lambda i: (i, 0)),
                  _const_spec(wo.shape), _const_spec((1, d))],
        out_specs=pl.BlockSpec((tm, d), lambda i: (i, 0)),
        out_shape=jax.ShapeDtypeStruct((n, d), F32),
        compiler_params=_params("arbitrary"),
        name="attn_out",
    )(x, o, wo, gpost)


def _sample_q_body(q_ref, wukt_ref, ql_ref, qr_ref, *, n_heads, qk_nope, seq):
    n = q_ref.shape[0]
    lane = lax.broadcasted_iota(jnp.int32, (LANES, LANES), 1)
    src = lax.broadcasted_iota(jnp.int32, (LANES, LANES), 0)
    sel = (src == lane + qk_nope).astype(BF16)
    for hd in range(n_heads):
        qh = q_ref[:, hd * LANES:(hd + 1) * LANES]
        ql = jnp.dot(qh, wukt_ref[hd], preferred_element_type=F32)
        qr = jnp.dot(qh, sel, preferred_element_type=F32)
        ql_ref[:, hd] = ql.reshape(n // seq, seq, ql.shape[-1])
        qr_ref[:, hd] = qr.reshape(n // seq, seq, LANES)


def _sample_q(q, wuk_t, dec_batch, dec_seq, dm):
    c = dm.kv_lora
    body = functools.partial(_sample_q_body, n_heads=dm.n_heads, qk_nope=dm.qk_nope, seq=dec_seq)
    return pl.pallas_call(
        body,
        out_shape=[jax.ShapeDtypeStruct((dec_batch, dm.n_heads, dec_seq, c), F32),
                   jax.ShapeDtypeStruct((dec_batch, dm.n_heads, dec_seq, LANES), F32)],
        compiler_params=_params(),
        name="sample_q",
    )(q, wuk_t)


def _sample_attn_body(pt_ref, ql_ref, qr_ref, ckvn_ref, kpen_ref, *rest, n_pages, seq, scale, qk_rope):
    del pt_ref
    ckv_refs = rest[:n_pages]
    kpe_refs = rest[n_pages:2 * n_pages]
    o_ref, m_ref, l_ref, acc_ref, s_ref, cb_ref, kp_ref = rest[2 * n_pages:]
    j = pl.program_id(1)
    rows, c = acc_ref.shape
    page = ckv_refs[0].shape[1]
    ql = ql_ref[0].reshape(rows, c).astype(BF16)
    qr = qr_ref[0].reshape(rows, LANES).astype(BF16)
    nt = (((1,), (1,)), ((), ()))

    def scores(cb, kb):
        return (lax.dot_general(ql, cb, nt, preferred_element_type=F32)
                + lax.dot_general(qr, kb, nt, preferred_element_type=F32)) * scale

    @pl.when(j == 0)
    def _():
        kp_ref[...] = jnp.zeros_like(kp_ref)
        kp_ref[0:seq, 0:qk_rope] = kpen_ref[0]
        cb = jnp.concatenate([ckvn_ref[0], jnp.zeros((page - seq, c), F32)], axis=0).astype(BF16)
        s = scores(cb, kp_ref[...].astype(BF16))
        r = lax.broadcasted_iota(jnp.int32, s.shape, 0)
        k = lax.broadcasted_iota(jnp.int32, s.shape, 1)
        s = jnp.where(k <= r % seq, s, -jnp.inf)
        m = jnp.max(s, axis=-1, keepdims=True)
        p = jnp.exp(s - m)
        m_ref[...] = m
        l_ref[...] = jnp.sum(p, axis=-1, keepdims=True)
        acc_ref[...] = jnp.dot(p.astype(BF16), cb, preferred_element_type=F32)

    for g in range(n_pages):
        cb_ref[g] = ckv_refs[g][0].astype(BF16)
        kp_ref[:, 0:qk_rope] = kpe_refs[g][0]
        s_ref[:, g * page:(g + 1) * page] = scores(cb_ref[g], kp_ref[...].astype(BF16))
    s = s_ref[...]
    m = m_ref[...]
    m_new = jnp.maximum(m, jnp.max(s, axis=-1, keepdims=True))
    alpha = jnp.exp(m - m_new)
    p = jnp.exp(s - m_new)
    l_ref[...] = alpha * l_ref[...] + jnp.sum(p, axis=-1, keepdims=True)
    m_ref[...] = m_new
    pv = None
    for g in range(n_pages):
        part = jnp.dot(p[:, g * page:(g + 1) * page].astype(BF16), cb_ref[g], preferred_element_type=F32)
        pv = part if pv is None else pv + part
    acc_ref[...] = alpha * acc_ref[...] + pv

    @pl.when(j == pl.num_programs(1) - 1)
    def _():
        o_ref[0] = acc_ref[...] / l_ref[...]


def _sample_attn(q_lat, q_rope, ckv_new, kpe_new, cache_ckv, cache_kpe, page_table, dm):
    bd, hn, seq, c = q_lat.shape
    n_log = page_table.shape[1]
    pps = min(dm.pages_per_step, n_log)
    page = cache_ckv.shape[1]
    rope = cache_kpe.shape[2]
    rows = hn * seq
    scale = (dm.qk_nope + dm.qk_rope) ** -0.5
    body = functools.partial(_sample_attn_body, n_pages=pps, seq=seq, scale=scale, qk_rope=rope)

    def page_spec(width, g):
        return pl.BlockSpec((1, page, width), lambda b, j, pt, g=g: (pt[b, j * pps + g], 0, 0))

    grid_spec = pltpu.PrefetchScalarGridSpec(
        num_scalar_prefetch=1,
        grid=(bd, n_log // pps),
        in_specs=[pl.BlockSpec((1, hn, seq, c), lambda b, j, pt: (b, 0, 0, 0)),
                  pl.BlockSpec((1, hn, seq, LANES), lambda b, j, pt: (b, 0, 0, 0)),
                  pl.BlockSpec((1, seq, c), lambda b, j, pt: (b, 0, 0)),
                  pl.BlockSpec((1, seq, rope), lambda b, j, pt: (b, 0, 0))]
                 + [page_spec(c, g) for g in range(pps)] + [page_spec(rope, g) for g in range(pps)],
        out_specs=pl.BlockSpec((1, rows, c), lambda b, j, pt: (b, 0, 0)),
        scratch_shapes=[pltpu.VMEM((rows, 1), F32), pltpu.VMEM((rows, 1), F32), pltpu.VMEM((rows, c), F32),
                        pltpu.VMEM((rows, pps * page), F32), pltpu.VMEM((pps, page, c), BF16),
                        pltpu.VMEM((page, LANES), F32)],
    )
    return pl.pallas_call(
        body,
        grid_spec=grid_spec,
        out_shape=jax.ShapeDtypeStruct((bd, rows, c), F32),
        compiler_params=_params("arbitrary", "arbitrary"),
        name="attn_sample",
    )(page_table, q_lat, q_rope, ckv_new, kpe_new, *([cache_ckv] * pps), *([cache_kpe] * pps))


def _sample_out_body(x_ref, ol_ref, wuv_ref, wo_ref, gpost_ref, y_ref, *, n_heads):
    bb, _, seq, c = ol_ref.shape
    h = None
    for hd in range(n_heads):
        ol = ol_ref[:, hd].reshape(bb * seq, c).astype(BF16)
        oh = jnp.dot(ol, wuv_ref[hd], preferred_element_type=F32).astype(BF16)
        part = jnp.dot(oh, wo_ref[hd], preferred_element_type=F32)
        h = part if h is None else h + part
    y_ref[...] = x_ref[...] + _rms(h, gpost_ref[...])


def _sample_out(x, o_lat, wuv_p, wo_p, gpost, dm):
    n, d = x.shape
    bd, hn, seq, c = o_lat.shape
    bb = min(dm.out_bb, bd)
    body = functools.partial(_sample_out_body, n_heads=hn)
    return pl.pallas_call(
        body,
        grid=(bd // bb,),
        in_specs=[pl.BlockSpec((bb * seq, d), lambda i: (i, 0)),
                  pl.BlockSpec((bb, hn, seq, c), lambda i: (i, 0, 0, 0)),
                  _const_spec(wuv_p.shape), _const_spec(wo_p.shape), _const_spec((1, d))],
        out_specs=pl.BlockSpec((bb * seq, d), lambda i: (i, 0)),
        out_shape=jax.ShapeDtypeStruct((n, d), F32),
        compiler_params=_params("arbitrary"),
        name="sample_out",
    )(x, o_lat, wuv_p, wo_p, gpost)


def _forward(x, pos_tab, h0_re, h0_im, attend, w, dm, *, s5_tc, s5_bb):
    b, t, d = x.shape
    n = b * t
    xf = x.reshape(n, d)
    g = lambda a, i, k: a[i, k][None, :]
    xf = _ffn(xf, g(w['norm_pre'], 0, 0), g(w['norm_post'], 0, 0), *w['ffn'][0][0], dm)
    xt = xf.reshape(b, t, d).transpose(1, 0, 2)
    yt, hfin = _s5(xt, _state_to_slabs(h0_re, h0_im), g(w['norm_pre'], 0, 1), g(w['norm_post'], 0, 1),
                   w['bblk'], w['cw'], w['a_re'], w['a_im'], w['ssm_d'], w['wglu'], dm, tc=s5_tc, bb=s5_bb)
    xf = yt.transpose(1, 0, 2).reshape(n, d)
    s_re, s_im = _slabs_to_state(hfin, d // dm.ssm_group, dm.ssm_state)
    xf = _ffn(xf, g(w['norm_pre'], 0, 2), g(w['norm_post'], 0, 2), *w['ffn'][0][1], dm)
    xf = _ffn(xf, g(w['norm_pre'], 1, 0), g(w['norm_post'], 1, 0), *w['ffn'][1][0], dm)
    xf, c_kv, k_pe = attend(xf, pos_tab, g(w['norm_pre'], 1, 1), g(w['norm_post'], 1, 1))
    xf = _ffn(xf, g(w['norm_pre'], 1, 2), g(w['norm_post'], 1, 2), *w['ffn'][1][1], dm)
    return (xf.reshape(b, t, d), s_re[None], s_im[None],
            c_kv.reshape(1, b, t, -1), k_pe.reshape(1, b, t, -1))


def _run(dm, x_prompt, x_sample, state_ssm_re, state_ssm_im, cache_kv_latent, cache_k_rope, page_table,
         norm_pre, norm_post, ffn_w_gate, ffn_w_up, ffn_w_down,
         ssm_a_re, ssm_a_im, ssm_log_dt, ssm_b_re, ssm_b_im, ssm_c_re, ssm_c_im, ssm_d, ssm_w_glu,
         mla_w_in, mla_q_norm, mla_kv_norm, mla_w_uq, mla_w_ukv, mla_w_o):
    b_p, s_p, d = x_prompt.shape
    b_s, s_s, _ = x_sample.shape
    past_len = page_table.shape[1] * dm.page_size
    depth = norm_pre.shape[0]
    assert depth == 2 and ssm_a_re.shape[0] == 1 and mla_w_in.shape[0] == 1

    ab_re, ab_im, bb_re, bb_im = _ssm_prep(ssm_a_re[0], ssm_a_im[0], ssm_log_dt[0], ssm_b_re[0], ssm_b_im[0])
    bblk, cw, a_re, a_im = _s5_weights(bb_re, bb_im, ssm_c_re[0], ssm_c_im[0], ab_re, ab_im, dm)
    mw = _mla_weights(mla_w_in[0], mla_w_uq[0], mla_w_ukv[0], mla_w_o[0], dm)
    w = dict(norm_pre=norm_pre, norm_post=norm_post,
             ffn=[[(ffn_w_gate[i, k].astype(BF16), ffn_w_up[i, k].astype(BF16), ffn_w_down[i, k].astype(BF16))
                   for k in range(2)] for i in range(depth)],
             bblk=bblk, cw=cw, a_re=a_re, a_im=a_im, ssm_d=ssm_d[0][None, :], wglu=ssm_w_glu[0].astype(BF16))
    qn = mla_q_norm[0][None, :]
    kvn = mla_kv_norm[0][None, :]

    def prompt_attend(xf, tabs, gpre, gpost):
        q, c_kv, k_pe, kmat, vmat = _mla_proj(xf, gpre, mw['win'], qn, kvn, mw['wuq'], mw['wuk'], mw['wuv'],
                                              *tabs, dm, with_kv=True)
        o = _attn_prompt(q, kmat, vmat, b_p, s_p, dm)
        return _attn_out(xf, o, mw['wo'], gpost, dm), c_kv, k_pe

    def sample_attend(xf, tabs, gpre, gpost):
        q, c_kv, k_pe = _mla_proj(xf, gpre, mw['win'], qn, kvn, mw['wuq'], mw['wuk'], mw['wuv'],
                                  *tabs, dm, with_kv=False)
        q_lat, q_rope = _sample_q(q, mw['wuk_t'], b_s, s_s, dm)
        o_lat = _sample_attn(q_lat, q_rope, c_kv.reshape(b_s, s_s, -1), k_pe.reshape(b_s, s_s, -1),
                             cache_kv_latent[0], cache_k_rope[0], page_table, dm)
        o_lat = o_lat.reshape(b_s, dm.n_heads, s_s, -1)
        return _sample_out(xf, o_lat, mw['wuv_p'], mw['wo_p'], gpost, dm), c_kv, k_pe

    tm_p = min(dm.tm, b_p * s_p)
    tabs_p = _rope_tables(jnp.arange(max(s_p, tm_p), dtype=jnp.int32) % s_p, dm)
    h0 = jnp.zeros((b_p,) + state_ssm_re.shape[2:], F32)
    tc_p = max(dm.s5_rows // b_p, 1)
    out_p = _forward(x_prompt, tabs_p, h0, h0, prompt_attend, w, dm, s5_tc=min(tc_p, s_p), s5_bb=b_p)
    tm_s = min(dm.tm, b_s * s_s)
    tabs_s = _rope_tables(past_len + jnp.arange(max(s_s, tm_s), dtype=jnp.int32) % s_s, dm)
    bb_s = min(max(dm.s5_rows // s_s, SUBLANES), b_s)
    out_s = _forward(x_sample, tabs_s, state_ssm_re[0], state_ssm_im[0], sample_attend, w, dm,
                     s5_tc=s_s, s5_bb=bb_s)
    y_p, re_p, im_p, lat_p, kpe_p = out_p
    y_s, re_s, im_s, lat_s, kpe_s = out_s
    return (y_p, y_s, re_p, im_p, re_s, im_s, lat_p, kpe_p, lat_s, kpe_s)


def kernel(x_prompt, x_sample, state_ssm_re, state_ssm_im, cache_kv_latent, cache_k_rope, page_table,
           norm_pre, norm_post, ffn_w_gate, ffn_w_up, ffn_w_down,
           ssm_a_re, ssm_a_im, ssm_log_dt, ssm_b_re, ssm_b_im, ssm_c_re, ssm_c_im, ssm_d, ssm_w_glu,
           mla_w_in, mla_q_norm, mla_kv_norm, mla_w_uq, mla_w_ukv, mla_w_o):
    return _run(DIMS, x_prompt, x_sample, state_ssm_re, state_ssm_im, cache_kv_latent, cache_k_rope, page_table,
                norm_pre, norm_post, ffn_w_gate, ffn_w_up, ffn_w_down,
                ssm_a_re, ssm_a_im, ssm_log_dt, ssm_b_re, ssm_b_im, ssm_c_re, ssm_c_im, ssm_d, ssm_w_glu,
                mla_w_in, mla_q_norm, mla_kv_norm, mla_w_uq, mla_w_ukv, mla_w_o)
```

```python
import functools
import math
from typing import NamedTuple

import jax
import jax.numpy as jnp
from jax import lax
from jax.experimental import pallas as pl
from jax.experimental.pallas import tpu as pltpu

F32 = jnp.float32
BF16 = jnp.bfloat16

LANES = 128
VT_ROWS = 80
SUBLANES = 8
MXU_DIM = 256
VMEM_LIMIT_BYTES = 56 * 1024 * 1024

EPS = 1e-6
ROPE_THETA = 10000.0


class Dims(NamedTuple):
    d_model: int = 1024
    d_ff: int = 2816
    ssm_group: int = 16
    ssm_state: int = 64
    n_heads: int = 16
    qk_nope: int = 64
    qk_rope: int = 32
    v_head: int = 64
    kv_lora: int = 256
    q_lora: int = 768
    page_size: int = 128
    tm: int = 512
    ffn_chunk: int = 1408
    s5_rows: int = 256
    pages_per_step: int = 16
    out_bb: int = 32


DIMS = Dims()


def _rms(x, g):
    ms = jnp.mean(x * x, axis=-1, keepdims=True)
    return x * lax.rsqrt(ms + EPS) * g


def _const_spec(shape):
    zeros = (0,) * len(shape)
    return pl.BlockSpec(shape, lambda *_: zeros, pipeline_mode=pl.Buffered(1))


def _params(*sem):
    return pltpu.CompilerParams(dimension_semantics=sem, vmem_limit_bytes=VMEM_LIMIT_BYTES)


def _ffn_body(x_ref, gpre_ref, gpost_ref, wg_ref, wu_ref, wd_ref, o_ref, *, d_ff, chunk):
    x = x_ref[...]
    h = _rms(x, gpre_ref[...]).astype(BF16)
    y = None
    for c in range(d_ff // chunk):
        sl = slice(c * chunk, (c + 1) * chunk)
        g = jnp.dot(h, wg_ref[:, sl], preferred_element_type=F32)
        u = jnp.dot(h, wu_ref[:, sl], preferred_element_type=F32)
        a = (g * jax.nn.sigmoid(g) * u).astype(BF16)
        part = jnp.dot(a, wd_ref[sl, :], preferred_element_type=F32)
        y = part if y is None else y + part
    o_ref[...] = x + 0.5 * _rms(y, gpost_ref[...])


def _ffn(x, gpre, gpost, wg, wu, wd, dm):
    n, d = x.shape
    tm = min(dm.tm, n)
    body = functools.partial(_ffn_body, d_ff=dm.d_ff, chunk=dm.ffn_chunk)
    return pl.pallas_call(
        body,
        grid=(n // tm,),
        in_specs=[pl.BlockSpec((tm, d), lambda i: (i, 0)),
                  _const_spec((1, d)), _const_spec((1, d)),
                  _const_spec(wg.shape), _const_spec(wu.shape), _const_spec(wd.shape)],
        out_specs=pl.BlockSpec((tm, d), lambda i: (i, 0)),
        out_shape=jax.ShapeDtypeStruct((n, d), F32),
        compiler_params=_params("arbitrary"),
        name="ffn",
    )(x, gpre, gpost, wg, wu, wd)


def _ssm_prep_body(are_ref, aim_ref, ldt_ref, bre_ref, bim_ref, abre_ref, abim_ref, bbre_ref, bbim_ref):
    dt = jnp.exp(ldt_ref[...])
    lr = are_ref[...]
    li = aim_ref[...]
    mag = jnp.exp(lr * dt)
    ang = li * dt
    ab_re = mag * jnp.cos(ang)
    ab_im = mag * jnp.sin(ang)
    nr = ab_re - 1.0
    den = lr * lr + li * li
    g_re = (nr * lr + ab_im * li) / den
    g_im = (ab_im * lr - nr * li) / den
    abre_ref[...] = ab_re
    abim_ref[...] = ab_im
    b_re = bre_ref[...]
    b_im = bim_ref[...]
    bbre_ref[...] = g_re * b_re - g_im * b_im
    bbim_ref[...] = g_re * b_im + g_im * b_re


def _ssm_prep(a_re, a_im, log_dt, b_re, b_im):
    g, p = a_re.shape
    i = b_re.shape[-1]
    n = g * p
    col = lambda a: a.reshape(n, 1)
    ldt = jnp.broadcast_to(log_dt[:, None], (g, p))
    shp = [jax.ShapeDtypeStruct((n, 1), F32)] * 2 + [jax.ShapeDtypeStruct((n, i), F32)] * 2
    ab_re, ab_im, bb_re, bb_im = pl.pallas_call(_ssm_prep_body, out_shape=shp, name="ssm_prep")(
        col(a_re), col(a_im), col(ldt), b_re.reshape(n, i), b_im.reshape(n, i))
    return ab_re.reshape(g, p), ab_im.reshape(g, p), bb_re.reshape(g, p, i), bb_im.reshape(g, p, i)


def _s5_body(x_ref, h0_ref, perm_ref, permt_ref, gpre_ref, gpost_ref, bblk_ref, cw_ref, are_ref, aim_ref, d_ref,
             wglu_ref, o_ref, hfin_ref, s_ref, hst_ref, *, tc, bb, n_ktile, slabs_per_ktile, slabs_per_out):
    ti = pl.program_id(1)
    rows = tc * bb
    d = x_ref.shape[-1]
    n_slab = s_ref.shape[0]
    x = x_ref[...].reshape(rows, d)
    u_bt = _rms(x, gpre_ref[...])
    hi = u_bt.astype(BF16)
    rest = u_bt - hi.astype(F32)
    mid = rest.astype(BF16)
    lo = (rest - mid.astype(F32)).astype(BF16)
    perm = perm_ref[...]
    ub_f32 = jnp.dot(perm, hi, preferred_element_type=F32)
    ub = ub_f32.astype(BF16)
    u = ub_f32 + jnp.dot(perm, mid, preferred_element_type=F32) + jnp.dot(perm, lo, preferred_element_type=F32)
    for j in range(n_ktile):
        bu = jnp.dot(ub[:, j * MXU_DIM:(j + 1) * MXU_DIM], bblk_ref[j], preferred_element_type=F32)
        for q in range(slabs_per_ktile):
            s_ref[j * slabs_per_ktile + q] = bu[:, q * 2 * LANES:(q + 1) * 2 * LANES]

    @pl.when(ti == 0)
    def _():
        hst_ref[...] = h0_ref[...]

    n_par = 4
    n_sub = bb // SUBLANES

    def slab_group(kg, carry):
        ks = [kg * n_par + q for q in range(n_par)]
        ar = [jnp.broadcast_to(are_ref[k], (SUBLANES, LANES)) for k in ks]
        ai = [jnp.broadcast_to(aim_ref[k], (SUBLANES, LANES)) for k in ks]

        def sub_group(sg, carry2):
            r0 = pl.multiple_of(sg * SUBLANES, SUBLANES)
            hr = tuple(hst_ref[k, pl.ds(r0, SUBLANES), 0:LANES] for k in ks)
            hi = tuple(hst_ref[k, pl.ds(r0, SUBLANES), LANES:2 * LANES] for k in ks)

            def step(t, h):
                hr, hi = h
                r = pl.multiple_of(t * bb + r0, SUBLANES)
                nr, ni = [], []
                for q, k in enumerate(ks):
                    xr = s_ref[k, pl.ds(r, SUBLANES), 0:LANES]
                    xi = s_ref[k, pl.ds(r, SUBLANES), LANES:2 * LANES]
                    vr = ar[q] * hr[q] - ai[q] * hi[q] + xr
                    vi = ar[q] * hi[q] + ai[q] * hr[q] + xi
                    s_ref[k, pl.ds(r, SUBLANES), 0:LANES] = vr
                    s_ref[k, pl.ds(r, SUBLANES), LANES:2 * LANES] = vi
                    nr.append(vr)
                    ni.append(vi)
                return tuple(nr), tuple(ni)

            hr, hi = lax.fori_loop(0, tc, step, (hr, hi), unroll=min(tc, 8))
            for q, k in enumerate(ks):
                hst_ref[k, pl.ds(r0, SUBLANES), 0:LANES] = hr[q]
                hst_ref[k, pl.ds(r0, SUBLANES), LANES:2 * LANES] = hi[q]
            return carry2

        lax.fori_loop(0, n_sub, sub_group, 0)
        return carry

    lax.fori_loop(0, n_slab // n_par, slab_group, 0)
    hfin_ref[...] = hst_ref[...]

    pieces = []
    for j in range(n_slab // slabs_per_out):
        acc = None
        for q in range(slabs_per_out):
            k = j * slabs_per_out + q
            part = jnp.dot(s_ref[k].astype(BF16), cw_ref[k], preferred_element_type=F32)
            acc = part if acc is None else acc + part
        pieces.append(acc)
    y = jnp.concatenate(pieces, axis=-1) + d_ref[...] * u
    hg_tb = jax.nn.gelu(y).astype(BF16)
    hg = jnp.dot(permt_ref[...], hg_tb, preferred_element_type=F32).astype(BF16)
    z = jnp.dot(hg, wglu_ref[...], preferred_element_type=F32)
    mix = z[:, :d] * jax.nn.sigmoid(z[:, d:])
    o_ref[...] = (x + _rms(mix, gpost_ref[...])).reshape(bb, tc, d)


def _s5(x, h0, gpre, gpost, bblk, cw, a_re, a_im, dvec, wglu, dm, *, tc, bb):
    b, t, d = x.shape
    n_slab = cw.shape[0]
    n_ktile = bblk.shape[0]
    rows = tc * bb
    out_lanes_per_slab = (LANES // dm.ssm_state) * dm.ssm_group
    body = functools.partial(_s5_body, tc=tc, bb=bb, n_ktile=n_ktile, slabs_per_ktile=n_slab // n_ktile,
                             slabs_per_out=LANES // out_lanes_per_slab)
    r_out = jnp.arange(rows)
    perm = jax.nn.one_hot((r_out % bb) * tc + r_out // bb, rows, dtype=BF16)
    return pl.pallas_call(
        body,
        grid=(b // bb, t // tc),
        in_specs=[pl.BlockSpec((bb, tc, d), lambda bi, ti: (bi, ti, 0)),
                  pl.BlockSpec((n_slab, bb, 2 * LANES), lambda bi, ti: (0, bi, 0)),
                  _const_spec((rows, rows)), _const_spec((rows, rows)),
                  _const_spec((1, d)), _const_spec((1, d)),
                  _const_spec(bblk.shape), _const_spec(cw.shape),
                  _const_spec(a_re.shape), _const_spec(a_im.shape),
                  _const_spec((1, d)), _const_spec(wglu.shape)],
        out_specs=[pl.BlockSpec((bb, tc, d), lambda bi, ti: (bi, ti, 0)),
                   pl.BlockSpec((n_slab, bb, 2 * LANES), lambda bi, ti: (0, bi, 0))],
        out_shape=[jax.ShapeDtypeStruct((b, t, d), F32),
                   jax.ShapeDtypeStruct((n_slab, b, 2 * LANES), F32)],
        scratch_shapes=[pltpu.VMEM((n_slab, rows, 2 * LANES), F32),
                        pltpu.VMEM((n_slab, bb, 2 * LANES), F32)],
        compiler_params=_params("arbitrary", "arbitrary"),
        name="s5_mixer",
    )(x, h0, perm, perm.T, gpre, gpost, bblk, cw, a_re, a_im, dvec, wglu)


def _s5_weights(bb_re, bb_im, c_re, c_im, ab_re, ab_im, dm):
    g, p, i = bb_re.shape
    gps = LANES // p
    n_slab = g // gps
    g_tile = MXU_DIM // i
    n_ktile = g // g_tile
    eye_t = jnp.eye(g_tile, dtype=F32)

    def b_tiles(bm):
        bt = bm.transpose(0, 2, 1).reshape(n_ktile, g_tile, i, p)
        return jnp.einsum('jgip,gh->jgihp', bt, eye_t).reshape(n_ktile, g_tile * i, g_tile // gps, LANES)

    bblk = jnp.stack([b_tiles(bb_re), b_tiles(bb_im)], axis=3)
    bblk = bblk.reshape(n_ktile, g_tile * i, g_tile * p * 2).astype(BF16)

    spo = LANES // (gps * i)
    eye_g = jnp.eye(gps, dtype=F32)
    sel = jax.nn.one_hot(jnp.arange(n_slab) % spo, spo, dtype=F32)

    def c_slabs(cm):
        ct = cm.reshape(n_slab, gps, i, p)
        base = jnp.einsum('khip,hg->khpgi', ct, eye_g).reshape(n_slab, gps * p, gps * i)
        return jnp.einsum('krc,kq->krqc', base, sel).reshape(n_slab, LANES, LANES)

    cw = jnp.concatenate([c_slabs(c_re), -c_slabs(c_im)], axis=1).astype(BF16)
    a_re = ab_re.reshape(n_slab, 1, LANES)
    a_im = ab_im.reshape(n_slab, 1, LANES)
    return bblk, cw, a_re, a_im


def _state_to_slabs(h_re, h_im):
    b = h_re.shape[0]
    re = h_re.reshape(b, -1, LANES)
    im = h_im.reshape(b, -1, LANES)
    return jnp.concatenate([re, im], axis=-1).transpose(1, 0, 2)


def _slabs_to_state(hs, g, p):
    n_slab, b, _ = hs.shape
    hs = hs.transpose(1, 0, 2)
    re = hs[:, :, :LANES].reshape(b, g, p)
    im = hs[:, :, LANES:].reshape(b, g, p)
    return re, im


def _q_scale(dm):
    return (dm.qk_nope + dm.qk_rope) ** -0.5 * math.log2(math.e)


def _rope_block(blk, t1, t2):
    return blk * t1 + pltpu.roll(blk, 3 * LANES // 4, 1) * t2


def _mla_proj_body(x_ref, gpre_ref, win_ref, qn_ref, kvn_ref, wuq_ref, wuk_ref, wuv_ref, vone_ref, t1_ref, t2_ref,
                   q_ref, ckv_ref, kpe_ref, *kv_refs, q_lora, kv_lora, n_heads, qk_rope, q_scale):
    x = x_ref[...]
    h = _rms(x, gpre_ref[...]).astype(BF16)
    hin = jnp.dot(h, win_ref[...], preferred_element_type=F32)
    cq = _rms(hin[:, :q_lora], qn_ref[...]).astype(BF16)
    ckv = _rms(hin[:, q_lora:q_lora + kv_lora], kvn_ref[...])
    t1 = t1_ref[...]
    t2 = t2_ref[...]
    kpe_blk = _rope_block(hin[:, q_lora + kv_lora:], t1, t2)
    ckv_ref[...] = ckv
    kpe_ref[...] = pltpu.roll(kpe_blk, LANES // 2, 1)[:, :qk_rope]
    q = jnp.dot(cq, wuq_ref[...], preferred_element_type=F32)
    for hd in range(n_heads):
        sl = slice(hd * LANES, (hd + 1) * LANES)
        q_ref[:, sl] = (_rope_block(q[:, sl], t1, t2) * q_scale).astype(BF16)
    if kv_refs:
        k_ref, v_ref = kv_refs
        cb = ckv.astype(BF16)
        kn = jnp.dot(cb, wuk_ref[...], preferred_element_type=F32)
        for hd in range(n_heads):
            sl = slice(hd * LANES, (hd + 1) * LANES)
            k_ref[:, sl] = (kn[:, sl] + kpe_blk).astype(BF16)
        vt = lax.dot_general(wuv_ref[...], cb, (((1,), (1,)), ((), ())), preferred_element_type=F32)
        v_ref[...] = (vt + vone_ref[...]).astype(BF16)


def _mla_proj(x, gpre, win, qn, kvn, wuq, wuk, wuv, vone, t1, t2, dm, *, with_kv):
    n, d = x.shape
    tm = min(dm.tm, n)
    n_tab = t1.shape[0] // tm
    hq = dm.n_heads * LANES
    hv = wuv.shape[0]
    body = functools.partial(_mla_proj_body, q_lora=dm.q_lora, kv_lora=dm.kv_lora, n_heads=dm.n_heads,
                             qk_rope=dm.qk_rope, q_scale=_q_scale(dm))
    row = lambda w: pl.BlockSpec((tm, w), lambda i: (i, 0))
    tab = pl.BlockSpec((tm, LANES), lambda i: (i % n_tab, 0))
    out_specs = [row(hq), row(dm.kv_lora), row(dm.qk_rope)]
    out_shape = [jax.ShapeDtypeStruct((n, hq), BF16), jax.ShapeDtypeStruct((n, dm.kv_lora), F32),
                 jax.ShapeDtypeStruct((n, dm.qk_rope), F32)]
    if with_kv:
        out_specs += [row(hq), pl.BlockSpec((None, hv, tm), lambda i: (i, 0, 0))]
        out_shape += [jax.ShapeDtypeStruct((n, hq), BF16), jax.ShapeDtypeStruct((n // tm, hv, tm), BF16)]
    return pl.pallas_call(
        body,
        grid=(n // tm,),
        in_specs=[row(d), _const_spec((1, d)), _const_spec(win.shape), _const_spec(qn.shape),
                  _const_spec(kvn.shape), _const_spec(wuq.shape), _const_spec(wuk.shape),
                  _const_spec(wuv.shape), _const_spec(vone.shape), tab, tab],
        out_specs=out_specs,
        out_shape=out_shape,
        compiler_params=_params("arbitrary"),
        name="mla_proj_kv" if with_kv else "mla_proj",
    )(x, gpre, win, qn, kvn, wuq, wuk, wuv, vone, t1, t2)


def _rope_tables(pos, dm):
    half = dm.qk_rope // 2
    inv = ROPE_THETA ** (-jnp.arange(0, dm.qk_rope, 2, dtype=F32) / dm.qk_rope)
    ang = pos.astype(F32)[:, None] * inv[None, :]
    cos, sin = jnp.cos(ang), jnp.sin(ang)
    n = pos.shape[0]
    pad = LANES - dm.qk_nope - dm.qk_rope
    t1 = jnp.concatenate([jnp.ones((n, dm.qk_nope), F32), cos, cos, jnp.zeros((n, pad), F32)], axis=1)
    t2 = jnp.concatenate([jnp.zeros((n, dm.qk_nope), F32), -sin, sin, jnp.zeros((n, pad), F32)], axis=1)
    del half
    return t1, t2


def _mla_weights(w_in, w_uq, w_ukv, w_o, dm):
    hn, nope, rope, vh = dm.n_heads, dm.qk_nope, dm.qk_rope, dm.v_head
    half = rope // 2
    d = w_in.shape[0]
    lat = dm.q_lora + dm.kv_lora
    kpe = w_in[:, lat:]
    kpe_sw = jnp.concatenate([kpe[:, half:], kpe[:, :half]], axis=1)
    win = jnp.concatenate([w_in[:, :lat], jnp.zeros((d, nope), F32), kpe, kpe_sw], axis=1).astype(BF16)
    uq = w_uq.reshape(dm.q_lora, hn, nope + rope)
    wuq = jnp.concatenate([uq, uq[:, :, nope + half:], uq[:, :, nope:nope + half]], axis=2)
    wuq = wuq.reshape(dm.q_lora, hn * LANES).astype(BF16)
    ukv = w_ukv.reshape(dm.kv_lora, hn, nope + vh)
    wuk = jnp.concatenate([ukv[:, :, :nope], jnp.zeros((dm.kv_lora, hn, LANES - nope), F32)], axis=2)
    wuk = wuk.reshape(dm.kv_lora, hn * LANES).astype(BF16)
    wuv = jnp.concatenate([ukv[:, :, nope:].transpose(1, 2, 0), jnp.zeros((hn, VT_ROWS - vh, dm.kv_lora), F32)],
                          axis=1).reshape(hn * VT_ROWS, dm.kv_lora).astype(BF16)
    vone = jnp.tile((jnp.arange(VT_ROWS) == vh).astype(F32), hn)[:, None]
    wuk_t = jnp.concatenate([ukv[:, :, :nope].transpose(1, 2, 0),
                             jnp.zeros((hn, LANES - nope, dm.kv_lora), F32)], axis=1).astype(BF16)
    wuv_p = jnp.concatenate([ukv[:, :, nope:].transpose(1, 0, 2),
                             jnp.zeros((hn, dm.kv_lora, LANES - vh), F32)], axis=2).astype(BF16)
    wo_p = jnp.concatenate([w_o.reshape(hn, vh, d), jnp.zeros((hn, LANES - vh, d), F32)], axis=1).astype(BF16)
    return dict(win=win, wuq=wuq, wuk=wuk, wuv=wuv, vone=vone, wuk_t=wuk_t, wuv_p=wuv_p, wo_p=wo_p, wo=w_o.astype(BF16))


def _attn_prompt_body(q_ref, k_ref, vt_ref, o_ref, sa_ref, sb_ref, m_ref, acc_ref, *, tq, v_head):
    qi = pl.program_id(2)
    n_blocks = pl.num_programs(2)
    n_hh = q_ref.shape[-1] // LANES
    key = lax.broadcasted_iota(jnp.int32, (tq, tq), 0)
    qry = lax.broadcasted_iota(jnp.int32, (tq, tq), 1)
    nt = (((1,), (1,)), ((), ()))
    qh = [q_ref[:, hh * LANES:(hh + 1) * LANES] for hh in range(n_hh)]

    def scores_into(dst_ref, ki):
        r = pl.multiple_of(ki * tq, tq)
        for hh in range(n_hh):
            kk = k_ref[pl.ds(r, tq), hh * LANES:(hh + 1) * LANES]
            dst_ref[hh] = lax.dot_general(kk, qh[hh], nt, preferred_element_type=F32)

    def accumulate(src_ref, ki, ki_loaded, masked):
        for hh in range(n_hh):
            st = src_ref[hh]
            if masked:
                st = jnp.where(key + ki * tq <= qry + qi * tq, st, -jnp.inf)
            m = m_ref[hh]
            m_new = jnp.maximum(m, jnp.max(st, axis=0, keepdims=True))
            alpha = jnp.exp2(m - m_new)
            pt = jnp.exp2(st - m_new).astype(BF16)
            m_ref[hh] = m_new
            vt = vt_ref[ki_loaded, hh * VT_ROWS:(hh + 1) * VT_ROWS, :]
            acc_ref[hh] = alpha * acc_ref[hh] + jnp.dot(vt, pt, preferred_element_type=F32)

    m_ref[...] = jnp.full(m_ref.shape, -jnp.inf, F32)
    acc_ref[...] = jnp.zeros(acc_ref.shape, F32)

    scores_into(sa_ref, 0)

    def pair(kp, carry):
        scores_into(sb_ref, 2 * kp + 1)
        accumulate(sa_ref, 2 * kp, 2 * kp, False)
        scores_into(sa_ref, 2 * kp + 2)
        accumulate(sb_ref, 2 * kp + 1, 2 * kp + 1, False)
        return carry

    lax.fori_loop(0, qi // 2, pair, 0)
    k0 = 2 * (qi // 2)
    k1 = jnp.minimum(k0 + 1, n_blocks - 1)
    scores_into(sb_ref, k1)
    accumulate(sa_ref, k0, k0, True)
    accumulate(sb_ref, k0 + 1, k1, True)
    ot = jnp.concatenate([acc_ref[hh, 0:v_head, :] / acc_ref[hh, v_head:v_head + 1, :] for hh in range(n_hh)],
                         axis=0)
    o_ref[...] = ot.T.astype(o_ref.dtype)


def _attn_prompt(q, k, vt, batch, seq, dm):
    tq = vt.shape[-1]
    nq = seq // tq
    hp = dm.n_heads // 2
    k3 = k.reshape(batch, seq, -1)
    vt4 = vt.reshape(batch, nq, vt.shape[1], tq)
    body = functools.partial(_attn_prompt_body, tq=tq, v_head=dm.v_head)
    return pl.pallas_call(
        body,
        grid=(batch, hp, nq),
        in_specs=[pl.BlockSpec((tq, 2 * LANES), lambda b, h, i: (b * nq + i, h)),
                  pl.BlockSpec((None, seq, 2 * LANES), lambda b, h, i: (b, 0, h)),
                  pl.BlockSpec((None, nq, 2 * VT_ROWS, tq), lambda b, h, i: (b, 0, h, 0))],
        out_specs=pl.BlockSpec((tq, 2 * dm.v_head), lambda b, h, i: (b * nq + i, h)),
        out_shape=jax.ShapeDtypeStruct((batch * seq, dm.n_heads * dm.v_head), BF16),
        scratch_shapes=[pltpu.VMEM((2, tq, tq), F32), pltpu.VMEM((2, tq, tq), F32),
                        pltpu.VMEM((2, 1, tq), F32), pltpu.VMEM((2, VT_ROWS, tq), F32)],
        compiler_params=_params("arbitrary", "arbitrary", "arbitrary"),
        name="attn_prompt",
    )(q, k3, vt4)


def _attn_out_body(x_ref, o_ref, wo_ref, gpost_ref, y_ref):
    h = jnp.dot(o_ref[...], wo_ref[...], preferred_element_type=F32)
    y_ref[...] = x_ref[...] + _rms(h, gpost_ref[...])


def _attn_out(x, o, wo, gpost, dm):
    n, d = x.shape
    tm = min(dm.tm, n)
    return pl.pallas_call(
        _attn_out_body,
        grid=(n // tm,),
        in_specs=[pl.BlockSpec((tm, d), lambda i: (i, 0)), pl.BlockSpec((tm, o.shape[1]), lambda i: (i, 0)),
                  _const_spec(wo.shape), _const_spec((1, d))],
        out_specs=pl.BlockSpec((tm, d), lambda i: (i, 0)),
        out_shape=jax.ShapeDtypeStruct((n, d), F32),
        compiler_params=_params("arbitrary"),
        name="attn_out",
    )(x, o, wo, gpost)


def _sample_q_body(q_ref, wukt_ref, ql_ref, qr_ref, *, n_heads, qk_nope, seq):
    n = q_ref.shape[0]
    lane = lax.broadcasted_iota(jnp.int32, (LANES, LANES), 1)
    src = lax.broadcasted_iota(jnp.int32, (LANES, LANES), 0)
    sel = (src == lane + qk_nope).astype(BF16)
    for hd in range(n_heads):
        qh = q_ref[:, hd * LANES:(hd + 1) * LANES]
        ql = jnp.dot(qh, wukt_ref[hd], preferred_element_type=F32)
        qr = jnp.dot(qh, sel, preferred_element_type=F32)
        ql_ref[:, hd] = ql.reshape(n // seq, seq, ql.shape[-1])
        qr_ref[:, hd] = qr.reshape(n // seq, seq, LANES)


def _sample_q(q, wuk_t, dec_batch, dec_seq, dm):
    c = dm.kv_lora
    body = functools.partial(_sample_q_body, n_heads=dm.n_heads, qk_nope=dm.qk_nope, seq=dec_seq)
    return pl.pallas_call(
        body,
        out_shape=[jax.ShapeDtypeStruct((dec_batch, dm.n_heads, dec_seq, c), F32),
                   jax.ShapeDtypeStruct((dec_batch, dm.n_heads, dec_seq, LANES), F32)],
        compiler_params=_params(),
        name="sample_q",
    )(q, wuk_t)


def _sample_attn_body(pt_ref, ql_ref, qr_ref, ckvn_ref, kpen_ref, *rest, n_pages, seq, qk_rope):
    del pt_ref
    ckv_refs = rest[:n_pages]
    kpe_refs = rest[n_pages:2 * n_pages]
    o_ref, m_ref, l_ref, acc_ref, cat_ref, kp_ref = rest[2 * n_pages:]
    j = pl.program_id(1)
    rows, c = acc_ref.shape
    page = ckv_refs[0].shape[1]
    q_cat = jnp.concatenate([ql_ref[0].reshape(rows, c), qr_ref[0].reshape(rows, LANES)], axis=-1).astype(BF16)
    nt = (((1,), (1,)), ((), ()))

    @pl.when(j == 0)
    def _():
        kp_ref[...] = jnp.zeros_like(kp_ref)
        kp_ref[0:seq, 0:qk_rope] = kpen_ref[0]
        cn = jnp.concatenate([ckvn_ref[0], jnp.zeros((page - seq, c), F32)], axis=0).astype(BF16)
        k_cat = jnp.concatenate([cn, kp_ref[0:page, :].astype(BF16)], axis=-1)
        s = lax.dot_general(q_cat, k_cat, nt, preferred_element_type=F32)
        r = lax.broadcasted_iota(jnp.int32, s.shape, 0)
        k = lax.broadcasted_iota(jnp.int32, s.shape, 1)
        s = jnp.where(k <= r % seq, s, -jnp.inf)
        m = jnp.max(s, axis=-1, keepdims=True)
        p = jnp.exp2(s - m)
        m_ref[...] = m
        l_ref[...] = jnp.sum(p, axis=-1, keepdims=True)
        acc_ref[...] = jnp.dot(p.astype(BF16), cn, preferred_element_type=F32)

    n_chunks = 4 if n_pages % 4 == 0 else 1
    cp = n_pages // n_chunks * page
    s = []
    for ch in range(n_chunks):
        for g in range(ch * n_pages // n_chunks, (ch + 1) * n_pages // n_chunks):
            cat_ref[g * page:(g + 1) * page, 0:c] = ckv_refs[g][0].astype(BF16)
            kp_ref[g * page:(g + 1) * page, 0:qk_rope] = kpe_refs[g][0]
        cat_ref[ch * cp:(ch + 1) * cp, c:c + LANES] = kp_ref[ch * cp:(ch + 1) * cp, :].astype(BF16)
        s.append(lax.dot_general(q_cat, cat_ref[ch * cp:(ch + 1) * cp, :], nt, preferred_element_type=F32))
    m = m_ref[...]
    m_new = m
    for sc in s:
        m_new = jnp.maximum(m_new, jnp.max(sc, axis=-1, keepdims=True))
    alpha = jnp.exp2(m - m_new)
    l = alpha * l_ref[...]
    acc = alpha * acc_ref[...]
    for ch, sc in enumerate(s):
        p = jnp.exp2(sc - m_new)
        l = l + jnp.sum(p, axis=-1, keepdims=True)
        acc = acc + jnp.dot(p.astype(BF16), cat_ref[ch * cp:(ch + 1) * cp, 0:c], preferred_element_type=F32)
    m_ref[...] = m_new
    l_ref[...] = l
    acc_ref[...] = acc

    @pl.when(j == pl.num_programs(1) - 1)
    def _():
        o_ref[0] = acc_ref[...] / l_ref[...]


def _sample_attn(q_lat, q_rope, ckv_new, kpe_new, cache_ckv, cache_kpe, page_table, dm):
    bd, hn, seq, c = q_lat.shape
    n_log = page_table.shape[1]
    pps = min(dm.pages_per_step, n_log)
    page = cache_ckv.shape[1]
    rope = cache_kpe.shape[2]
    rows = hn * seq
    body = functools.partial(_sample_attn_body, n_pages=pps, seq=seq, qk_rope=rope)

    def page_spec(width, g):
        return pl.BlockSpec((1, page, width), lambda b, j, pt, g=g: (pt[b, j * pps + g], 0, 0))

    grid_spec = pltpu.PrefetchScalarGridSpec(
        num_scalar_prefetch=1,
        grid=(bd, n_log // pps),
        in_specs=[pl.BlockSpec((1, hn, seq, c), lambda b, j, pt: (b, 0, 0, 0)),
                  pl.BlockSpec((1, hn, seq, LANES), lambda b, j, pt: (b, 0, 0, 0)),
                  pl.BlockSpec((1, seq, c), lambda b, j, pt: (b, 0, 0)),
                  pl.BlockSpec((1, seq, rope), lambda b, j, pt: (b, 0, 0))]
                 + [page_spec(c, g) for g in range(pps)] + [page_spec(rope, g) for g in range(pps)],
        out_specs=pl.BlockSpec((1, rows, c), lambda b, j, pt: (b, 0, 0)),
        scratch_shapes=[pltpu.VMEM((rows, 1), F32), pltpu.VMEM((rows, 1), F32), pltpu.VMEM((rows, c), F32),
                        pltpu.VMEM((pps * page, c + LANES), BF16), pltpu.VMEM((pps * page, LANES), F32)],
    )
    return pl.pallas_call(
        body,
        grid_spec=grid_spec,
        out_shape=jax.ShapeDtypeStruct((bd, rows, c), F32),
        compiler_params=_params("arbitrary", "arbitrary"),
        name="attn_sample",
    )(page_table, q_lat, q_rope, ckv_new, kpe_new, *([cache_ckv] * pps), *([cache_kpe] * pps))


def _sample_out_body(x_ref, ol_ref, wuv_ref, wo_ref, gpost_ref, y_ref, *, n_heads):
    bb, _, seq, c = ol_ref.shape
    h = None
    for hd in range(n_heads):
        ol = ol_ref[:, hd].reshape(bb * seq, c).astype(BF16)
        oh = jnp.dot(ol, wuv_ref[hd], preferred_element_type=F32).astype(BF16)
        part = jnp.dot(oh, wo_ref[hd], preferred_element_type=F32)
        h = part if h is None else h + part
    y_ref[...] = x_ref[...] + _rms(h, gpost_ref[...])


def _sample_out(x, o_lat, wuv_p, wo_p, gpost, dm):
    n, d = x.shape
    bd, hn, seq, c = o_lat.shape
    bb = min(dm.out_bb, bd)
    body = functools.partial(_sample_out_body, n_heads=hn)
    return pl.pallas_call(
        body,
        grid=(bd // bb,),
        in_specs=[pl.BlockSpec((bb * seq, d), lambda i: (i, 0)),
                  pl.BlockSpec((bb, hn, seq, c), lambda i: (i, 0, 0, 0)),
                  _const_spec(wuv_p.shape), _const_spec(wo_p.shape), _const_spec((1, d))],
        out_specs=pl.BlockSpec((bb * seq, d), lambda i: (i, 0)),
        out_shape=jax.ShapeDtypeStruct((n, d), F32),
        compiler_params=_params("arbitrary"),
        name="sample_out",
    )(x, o_lat, wuv_p, wo_p, gpost)


def _forward(x, pos_tab, h0_re, h0_im, attend, w, dm, *, s5_tc, s5_bb):
    b, t, d = x.shape
    n = b * t
    xf = x.reshape(n, d)
    g = lambda a, i, k: a[i, k][None, :]
    xf = _ffn(xf, g(w['norm_pre'], 0, 0), g(w['norm_post'], 0, 0), *w['ffn'][0][0], dm)
    y3, hfin = _s5(xf.reshape(b, t, d), _state_to_slabs(h0_re, h0_im), g(w['norm_pre'], 0, 1),
                   g(w['norm_post'], 0, 1), w['bblk'], w['cw'], w['a_re'], w['a_im'], w['ssm_d'], w['wglu'], dm,
                   tc=s5_tc, bb=s5_bb)
    xf = y3.reshape(n, d)
    s_re, s_im = _slabs_to_state(hfin, d // dm.ssm_group, dm.ssm_state)
    xf = _ffn(xf, g(w['norm_pre'], 0, 2), g(w['norm_post'], 0, 2), *w['ffn'][0][1], dm)
    xf = _ffn(xf, g(w['norm_pre'], 1, 0), g(w['norm_post'], 1, 0), *w['ffn'][1][0], dm)
    xf, c_kv, k_pe = attend(xf, pos_tab, g(w['norm_pre'], 1, 1), g(w['norm_post'], 1, 1))
    xf = _ffn(xf, g(w['norm_pre'], 1, 2), g(w['norm_post'], 1, 2), *w['ffn'][1][1], dm)
    return (xf.reshape(b, t, d), s_re[None], s_im[None],
            c_kv.reshape(1, b, t, -1), k_pe.reshape(1, b, t, -1))


def _run(dm, x_prompt, x_sample, state_ssm_re, state_ssm_im, cache_kv_latent, cache_k_rope, page_table,
         norm_pre, norm_post, ffn_w_gate, ffn_w_up, ffn_w_down,
         ssm_a_re, ssm_a_im, ssm_log_dt, ssm_b_re, ssm_b_im, ssm_c_re, ssm_c_im, ssm_d, ssm_w_glu,
         mla_w_in, mla_q_norm, mla_kv_norm, mla_w_uq, mla_w_ukv, mla_w_o):
    b_p, s_p, d = x_prompt.shape
    b_s, s_s, _ = x_sample.shape
    past_len = page_table.shape[1] * dm.page_size
    depth = norm_pre.shape[0]
    assert depth == 2 and ssm_a_re.shape[0] == 1 and mla_w_in.shape[0] == 1

    ab_re, ab_im, bb_re, bb_im = _ssm_prep(ssm_a_re[0], ssm_a_im[0], ssm_log_dt[0], ssm_b_re[0], ssm_b_im[0])
    bblk, cw, a_re, a_im = _s5_weights(bb_re, bb_im, ssm_c_re[0], ssm_c_im[0], ab_re, ab_im, dm)
    mw = _mla_weights(mla_w_in[0], mla_w_uq[0], mla_w_ukv[0], mla_w_o[0], dm)
    w = dict(norm_pre=norm_pre, norm_post=norm_post,
             ffn=[[(ffn_w_gate[i, k].astype(BF16), ffn_w_up[i, k].astype(BF16), ffn_w_down[i, k].astype(BF16))
                   for k in range(2)] for i in range(depth)],
             bblk=bblk, cw=cw, a_re=a_re, a_im=a_im, ssm_d=ssm_d[0][None, :], wglu=ssm_w_glu[0].astype(BF16))
    qn = mla_q_norm[0][None, :]
    kvn = mla_kv_norm[0][None, :]

    def prompt_attend(xf, tabs, gpre, gpost):
        q, c_kv, k_pe, kmat, vt = _mla_proj(xf, gpre, mw['win'], qn, kvn, mw['wuq'], mw['wuk'], mw['wuv'],
                                            mw['vone'], *tabs, dm, with_kv=True)
        o = _attn_prompt(q, kmat, vt, b_p, s_p, dm)
        return _attn_out(xf, o, mw['wo'], gpost, dm), c_kv, k_pe

    def sample_attend(xf, tabs, gpre, gpost):
        q, c_kv, k_pe = _mla_proj(xf, gpre, mw['win'], qn, kvn, mw['wuq'], mw['wuk'], mw['wuv'],
                                  mw['vone'], *tabs, dm, with_kv=False)
        q_lat, q_rope = _sample_q(q, mw['wuk_t'], b_s, s_s, dm)
        o_lat = _sample_attn(q_lat, q_rope, c_kv.reshape(b_s, s_s, -1), k_pe.reshape(b_s, s_s, -1),
                             cache_kv_latent[0], cache_k_rope[0], page_table, dm)
        o_lat = o_lat.reshape(b_s, dm.n_heads, s_s, -1)
        return _sample_out(xf, o_lat, mw['wuv_p'], mw['wo_p'], gpost, dm), c_kv, k_pe

    tm_p = min(dm.tm, b_p * s_p)
    tabs_p = _rope_tables(jnp.arange(max(s_p, tm_p), dtype=jnp.int32) % s_p, dm)
    h0 = jnp.zeros((b_p,) + state_ssm_re.shape[2:], F32)
    tc_p = max(dm.s5_rows // b_p, 1)
    out_p = _forward(x_prompt, tabs_p, h0, h0, prompt_attend, w, dm, s5_tc=min(tc_p, s_p), s5_bb=b_p)
    tm_s = min(dm.tm, b_s * s_s)
    tabs_s = _rope_tables(past_len + jnp.arange(max(s_s, tm_s), dtype=jnp.int32) % s_s, dm)
    bb_s = min(max(dm.s5_rows // s_s, SUBLANES), b_s)
    out_s = _forward(x_sample, tabs_s, state_ssm_re[0], state_ssm_im[0], sample_attend, w, dm,
                     s5_tc=s_s, s5_bb=bb_s)
    y_p, re_p, im_p, lat_p, kpe_p = out_p
    y_s, re_s, im_s, lat_s, kpe_s = out_s
    return (y_p, y_s, re_p, im_p, re_s, im_s, lat_p, kpe_p, lat_s, kpe_s)


def kernel(x_prompt, x_sample, state_ssm_re, state_ssm_im, cache_kv_latent, cache_k_rope, page_table,
           norm_pre, norm_post, ffn_w_gate, ffn_w_up, ffn_w_down,
           ssm_a_re, ssm_a_im, ssm_log_dt, ssm_b_re, ssm_b_im, ssm_c_re, ssm_c_im, ssm_d, ssm_w_glu,
           mla_w_in, mla_q_norm, mla_kv_norm, mla_w_uq, mla_w_ukv, mla_w_o):
    return _run(DIMS, x_prompt, x_sample, state_ssm_re, state_ssm_im, cache_kv_latent, cache_k_rope, page_table,
                norm_pre, norm_post, ffn_w_gate, ffn_w_up, ffn_w_down,
                ssm_a_re, ssm_a_im, ssm_log_dt, ssm_b_re, ssm_b_im, ssm_c_re, ssm_c_im, ssm_d, ssm_w_glu,
                mla_w_in, mla_q_norm, mla_kv_norm, mla_w_uq, mla_w_ukv, mla_w_o)
```

```python
import functools
import math
from typing import NamedTuple

import jax
import jax.numpy as jnp
from jax import lax
from jax.experimental import pallas as pl
from jax.experimental.pallas import tpu as pltpu

F32 = jnp.float32
BF16 = jnp.bfloat16

LANES = 128
VT_ROWS = 80
SUBLANES = 8
MXU_DIM = 256
VMEM_LIMIT_BYTES = 56 * 1024 * 1024

EPS = 1e-6
ROPE_THETA = 10000.0


class Dims(NamedTuple):
    d_model: int = 1024
    d_ff: int = 2816
    ssm_group: int = 16
    ssm_state: int = 64
    n_heads: int = 16
    qk_nope: int = 64
    qk_rope: int = 32
    v_head: int = 64
    kv_lora: int = 256
    q_lora: int = 768
    page_size: int = 128
    tm: int = 512
    ffn_chunk: int = 1408
    s5_rows: int = 256
    pages_per_step: int = 16
    out_bb: int = 32


DIMS = Dims()


def _rms(x, g):
    ms = jnp.mean(x * x, axis=-1, keepdims=True)
    return x * lax.rsqrt(ms + EPS) * g


def _const_spec(shape):
    zeros = (0,) * len(shape)
    return pl.BlockSpec(shape, lambda *_: zeros, pipeline_mode=pl.Buffered(1))


def _params(*sem):
    return pltpu.CompilerParams(dimension_semantics=sem, vmem_limit_bytes=VMEM_LIMIT_BYTES)


def _ffn_body(x_ref, gpre_ref, gpost_ref, wg_ref, wu_ref, wd_ref, o_ref, *, d_ff, chunk):
    x = x_ref[...]
    h = _rms(x, gpre_ref[...]).astype(BF16)
    y = None
    for c in range(d_ff // chunk):
        sl = slice(c * chunk, (c + 1) * chunk)
        g = jnp.dot(h, wg_ref[:, sl], preferred_element_type=F32)
        u = jnp.dot(h, wu_ref[:, sl], preferred_element_type=F32)
        a = (g * jax.nn.sigmoid(g) * u).astype(BF16)
        part = jnp.dot(a, wd_ref[sl, :], preferred_element_type=F32)
        y = part if y is None else y + part
    o_ref[...] = x + 0.5 * _rms(y, gpost_ref[...])


def _ffn(x, gpre, gpost, wg, wu, wd, dm):
    n, d = x.shape
    tm = min(dm.tm, n)
    body = functools.partial(_ffn_body, d_ff=dm.d_ff, chunk=dm.ffn_chunk)
    return pl.pallas_call(
        body,
        grid=(n // tm,),
        in_specs=[pl.BlockSpec((tm, d), lambda i: (i, 0)),
                  _const_spec((1, d)), _const_spec((1, d)),
                  _const_spec(wg.shape), _const_spec(wu.shape), _const_spec(wd.shape)],
        out_specs=pl.BlockSpec((tm, d), lambda i: (i, 0)),
        out_shape=jax.ShapeDtypeStruct((n, d), F32),
        compiler_params=_params("arbitrary"),
        name="ffn",
    )(x, gpre, gpost, wg, wu, wd)


def _ssm_prep_body(are_ref, aim_ref, ldt_ref, bre_ref, bim_ref, abre_ref, abim_ref, bbre_ref, bbim_ref):
    dt = jnp.exp(ldt_ref[...])
    lr = are_ref[...]
    li = aim_ref[...]
    mag = jnp.exp(lr * dt)
    ang = li * dt
    ab_re = mag * jnp.cos(ang)
    ab_im = mag * jnp.sin(ang)
    nr = ab_re - 1.0
    den = lr * lr + li * li
    g_re = (nr * lr + ab_im * li) / den
    g_im = (ab_im * lr - nr * li) / den
    abre_ref[...] = ab_re
    abim_ref[...] = ab_im
    b_re = bre_ref[...]
    b_im = bim_ref[...]
    bbre_ref[...] = g_re * b_re - g_im * b_im
    bbim_ref[...] = g_re * b_im + g_im * b_re


def _ssm_prep(a_re, a_im, log_dt, b_re, b_im):
    g, p = a_re.shape
    i = b_re.shape[-1]
    n = g * p
    col = lambda a: a.reshape(n, 1)
    ldt = jnp.broadcast_to(log_dt[:, None], (g, p))
    shp = [jax.ShapeDtypeStruct((n, 1), F32)] * 2 + [jax.ShapeDtypeStruct((n, i), F32)] * 2
    ab_re, ab_im, bb_re, bb_im = pl.pallas_call(_ssm_prep_body, out_shape=shp, name="ssm_prep")(
        col(a_re), col(a_im), col(ldt), b_re.reshape(n, i), b_im.reshape(n, i))
    return ab_re.reshape(g, p), ab_im.reshape(g, p), bb_re.reshape(g, p, i), bb_im.reshape(g, p, i)


def _s5_body(x_ref, h0_ref, perm_ref, permt_ref, gpre_ref, gpost_ref, bblk_ref, cw_ref, are_ref, aim_ref, d_ref,
             wglu_ref, o_ref, hfin_ref, s_ref, hst_ref, *, tc, bb, n_ktile, slabs_per_ktile, slabs_per_out):
    ti = pl.program_id(1)
    rows = tc * bb
    d = x_ref.shape[-1]
    n_slab = s_ref.shape[0]
    x = x_ref[...].reshape(rows, d)
    u_bt = _rms(x, gpre_ref[...])
    hi = u_bt.astype(BF16)
    rest = u_bt - hi.astype(F32)
    mid = rest.astype(BF16)
    lo = (rest - mid.astype(F32)).astype(BF16)
    perm = perm_ref[...]
    ub_f32 = jnp.dot(perm, hi, preferred_element_type=F32)
    ub = ub_f32.astype(BF16)
    u = ub_f32 + jnp.dot(perm, mid, preferred_element_type=F32) + jnp.dot(perm, lo, preferred_element_type=F32)
    for j in range(n_ktile):
        bu = jnp.dot(ub[:, j * MXU_DIM:(j + 1) * MXU_DIM], bblk_ref[j], preferred_element_type=F32)
        for q in range(slabs_per_ktile):
            s_ref[j * slabs_per_ktile + q] = bu[:, q * 2 * LANES:(q + 1) * 2 * LANES]

    @pl.when(ti == 0)
    def _():
        hst_ref[...] = h0_ref[...]

    n_par = 4
    n_sub = bb // SUBLANES

    def slab_group(kg, carry):
        ks = [kg * n_par + q for q in range(n_par)]
        ar = [jnp.broadcast_to(are_ref[k], (SUBLANES, LANES)) for k in ks]
        ai = [jnp.broadcast_to(aim_ref[k], (SUBLANES, LANES)) for k in ks]

        def sub_group(sg, carry2):
            r0 = pl.multiple_of(sg * SUBLANES, SUBLANES)
            hr = tuple(hst_ref[k, pl.ds(r0, SUBLANES), 0:LANES] for k in ks)
            hi = tuple(hst_ref[k, pl.ds(r0, SUBLANES), LANES:2 * LANES] for k in ks)

            def step(t, h):
                hr, hi = h
                r = pl.multiple_of(t * bb + r0, SUBLANES)
                nr, ni = [], []
                for q, k in enumerate(ks):
                    xr = s_ref[k, pl.ds(r, SUBLANES), 0:LANES]
                    xi = s_ref[k, pl.ds(r, SUBLANES), LANES:2 * LANES]
                    vr = ar[q] * hr[q] - ai[q] * hi[q] + xr
                    vi = ar[q] * hi[q] + ai[q] * hr[q] + xi
                    s_ref[k, pl.ds(r, SUBLANES), 0:LANES] = vr
                    s_ref[k, pl.ds(r, SUBLANES), LANES:2 * LANES] = vi
                    nr.append(vr)
                    ni.append(vi)
                return tuple(nr), tuple(ni)

            hr, hi = lax.fori_loop(0, tc, step, (hr, hi), unroll=min(tc, 8))
            for q, k in enumerate(ks):
                hst_ref[k, pl.ds(r0, SUBLANES), 0:LANES] = hr[q]
                hst_ref[k, pl.ds(r0, SUBLANES), LANES:2 * LANES] = hi[q]
            return carry2

        lax.fori_loop(0, n_sub, sub_group, 0)
        return carry

    lax.fori_loop(0, n_slab // n_par, slab_group, 0)
    hfin_ref[...] = hst_ref[...]

    pieces = []
    for j in range(n_slab // slabs_per_out):
        acc = None
        for q in range(slabs_per_out):
            k = j * slabs_per_out + q
            part = jnp.dot(s_ref[k].astype(BF16), cw_ref[k], preferred_element_type=F32)
            acc = part if acc is None else acc + part
        pieces.append(acc)
    y = jnp.concatenate(pieces, axis=-1) + d_ref[...] * u
    hg_tb = jax.nn.gelu(y).astype(BF16)
    hg = jnp.dot(permt_ref[...], hg_tb, preferred_element_type=F32).astype(BF16)
    z = jnp.dot(hg, wglu_ref[...], preferred_element_type=F32)
    mix = z[:, :d] * jax.nn.sigmoid(z[:, d:])
    o_ref[...] = (x + _rms(mix, gpost_ref[...])).reshape(bb, tc, d)


def _s5(x, h0, gpre, gpost, bblk, cw, a_re, a_im, dvec, wglu, dm, *, tc, bb):
    b, t, d = x.shape
    n_slab = cw.shape[0]
    n_ktile = bblk.shape[0]
    rows = tc * bb
    out_lanes_per_slab = (LANES // dm.ssm_state) * dm.ssm_group
    body = functools.partial(_s5_body, tc=tc, bb=bb, n_ktile=n_ktile, slabs_per_ktile=n_slab // n_ktile,
                             slabs_per_out=LANES // out_lanes_per_slab)
    r_out = jnp.arange(rows)
    perm = jax.nn.one_hot((r_out % bb) * tc + r_out // bb, rows, dtype=BF16)
    return pl.pallas_call(
        body,
        grid=(b // bb, t // tc),
        in_specs=[pl.BlockSpec((bb, tc, d), lambda bi, ti: (bi, ti, 0)),
                  pl.BlockSpec((n_slab, bb, 2 * LANES), lambda bi, ti: (0, bi, 0)),
                  _const_spec((rows, rows)), _const_spec((rows, rows)),
                  _const_spec((1, d)), _const_spec((1, d)),
                  _const_spec(bblk.shape), _const_spec(cw.shape),
                  _const_spec(a_re.shape), _const_spec(a_im.shape),
                  _const_spec((1, d)), _const_spec(wglu.shape)],
        out_specs=[pl.BlockSpec((bb, tc, d), lambda bi, ti: (bi, ti, 0)),
                   pl.BlockSpec((n_slab, bb, 2 * LANES), lambda bi, ti: (0, bi, 0))],
        out_shape=[jax.ShapeDtypeStruct((b, t, d), F32),
                   jax.ShapeDtypeStruct((n_slab, b, 2 * LANES), F32)],
        scratch_shapes=[pltpu.VMEM((n_slab, rows, 2 * LANES), F32),
                        pltpu.VMEM((n_slab, bb, 2 * LANES), F32)],
        compiler_params=_params("arbitrary", "arbitrary"),
        name="s5_mixer",
    )(x, h0, perm, perm.T, gpre, gpost, bblk, cw, a_re, a_im, dvec, wglu)


def _s5_weights(bb_re, bb_im, c_re, c_im, ab_re, ab_im, dm):
    g, p, i = bb_re.shape
    gps = LANES // p
    n_slab = g // gps
    g_tile = MXU_DIM // i
    n_ktile = g // g_tile
    eye_t = jnp.eye(g_tile, dtype=F32)

    def b_tiles(bm):
        bt = bm.transpose(0, 2, 1).reshape(n_ktile, g_tile, i, p)
        return jnp.einsum('jgip,gh->jgihp', bt, eye_t).reshape(n_ktile, g_tile * i, g_tile // gps, LANES)

    bblk = jnp.stack([b_tiles(bb_re), b_tiles(bb_im)], axis=3)
    bblk = bblk.reshape(n_ktile, g_tile * i, g_tile * p * 2).astype(BF16)

    spo = LANES // (gps * i)
    eye_g = jnp.eye(gps, dtype=F32)
    sel = jax.nn.one_hot(jnp.arange(n_slab) % spo, spo, dtype=F32)

    def c_slabs(cm):
        ct = cm.reshape(n_slab, gps, i, p)
        base = jnp.einsum('khip,hg->khpgi', ct, eye_g).reshape(n_slab, gps * p, gps * i)
        return jnp.einsum('krc,kq->krqc', base, sel).reshape(n_slab, LANES, LANES)

    cw = jnp.concatenate([c_slabs(c_re), -c_slabs(c_im)], axis=1).astype(BF16)
    a_re = ab_re.reshape(n_slab, 1, LANES)
    a_im = ab_im.reshape(n_slab, 1, LANES)
    return bblk, cw, a_re, a_im


def _state_to_slabs(h_re, h_im):
    b = h_re.shape[0]
    re = h_re.reshape(b, -1, LANES)
    im = h_im.reshape(b, -1, LANES)
    return jnp.concatenate([re, im], axis=-1).transpose(1, 0, 2)


def _slabs_to_state(hs, g, p):
    n_slab, b, _ = hs.shape
    hs = hs.transpose(1, 0, 2)
    re = hs[:, :, :LANES].reshape(b, g, p)
    im = hs[:, :, LANES:].reshape(b, g, p)
    return re, im


def _q_scale(dm):
    return (dm.qk_nope + dm.qk_rope) ** -0.5 * math.log2(math.e)


def _rope_block(blk, t1, t2):
    return blk * t1 + pltpu.roll(blk, 3 * LANES // 4, 1) * t2


def _mla_proj_body(x_ref, gpre_ref, win_ref, qn_ref, kvn_ref, wuq_ref, wuk_ref, wuv_ref, vone_ref, t1_ref, t2_ref,
                   q_ref, ckv_ref, kpe_ref, *kv_refs, q_lora, kv_lora, n_heads, qk_rope, q_scale):
    x = x_ref[...]
    h = _rms(x, gpre_ref[...]).astype(BF16)
    hin = jnp.dot(h, win_ref[...], preferred_element_type=F32)
    cq = _rms(hin[:, :q_lora], qn_ref[...]).astype(BF16)
    ckv = _rms(hin[:, q_lora:q_lora + kv_lora], kvn_ref[...])
    t1 = t1_ref[...]
    t2 = t2_ref[...]
    kpe_blk = _rope_block(hin[:, q_lora + kv_lora:], t1, t2)
    ckv_ref[...] = ckv
    kpe_ref[...] = pltpu.roll(kpe_blk, LANES // 2, 1)[:, :qk_rope]
    q = jnp.dot(cq, wuq_ref[...], preferred_element_type=F32)
    for hd in range(n_heads):
        sl = slice(hd * LANES, (hd + 1) * LANES)
        q_ref[:, sl] = (_rope_block(q[:, sl], t1, t2) * q_scale).astype(BF16)
    if kv_refs:
        k_ref, v_ref = kv_refs
        cb = ckv.astype(BF16)
        kn = jnp.dot(cb, wuk_ref[...], preferred_element_type=F32)
        for hd in range(n_heads):
            sl = slice(hd * LANES, (hd + 1) * LANES)
            k_ref[:, sl] = (kn[:, sl] + kpe_blk).astype(BF16)
        vt = lax.dot_general(wuv_ref[...], cb, (((1,), (1,)), ((), ())), preferred_element_type=F32)
        v_ref[...] = (vt + vone_ref[...]).astype(BF16)


def _mla_proj(x, gpre, win, qn, kvn, wuq, wuk, wuv, vone, t1, t2, dm, *, with_kv):
    n, d = x.shape
    tm = min(dm.tm, n)
    n_tab = t1.shape[0] // tm
    hq = dm.n_heads * LANES
    hv = wuv.shape[0]
    body = functools.partial(_mla_proj_body, q_lora=dm.q_lora, kv_lora=dm.kv_lora, n_heads=dm.n_heads,
                             qk_rope=dm.qk_rope, q_scale=_q_scale(dm))
    row = lambda w: pl.BlockSpec((tm, w), lambda i: (i, 0))
    tab = pl.BlockSpec((tm, LANES), lambda i: (i % n_tab, 0))
    out_specs = [row(hq), row(dm.kv_lora), row(dm.qk_rope)]
    out_shape = [jax.ShapeDtypeStruct((n, hq), BF16), jax.ShapeDtypeStruct((n, dm.kv_lora), F32),
                 jax.ShapeDtypeStruct((n, dm.qk_rope), F32)]
    if with_kv:
        out_specs += [row(hq), pl.BlockSpec((None, hv, tm), lambda i: (i, 0, 0))]
        out_shape += [jax.ShapeDtypeStruct((n, hq), BF16), jax.ShapeDtypeStruct((n // tm, hv, tm), BF16)]
    return pl.pallas_call(
        body,
        grid=(n // tm,),
        in_specs=[row(d), _const_spec((1, d)), _const_spec(win.shape), _const_spec(qn.shape),
                  _const_spec(kvn.shape), _const_spec(wuq.shape), _const_spec(wuk.shape),
                  _const_spec(wuv.shape), _const_spec(vone.shape), tab, tab],
        out_specs=out_specs,
        out_shape=out_shape,
        compiler_params=_params("arbitrary"),
        name="mla_proj_kv" if with_kv else "mla_proj",
    )(x, gpre, win, qn, kvn, wuq, wuk, wuv, vone, t1, t2)


def _rope_tables(pos, dm):
    half = dm.qk_rope // 2
    inv = ROPE_THETA ** (-jnp.arange(0, dm.qk_rope, 2, dtype=F32) / dm.qk_rope)
    ang = pos.astype(F32)[:, None] * inv[None, :]
    cos, sin = jnp.cos(ang), jnp.sin(ang)
    n = pos.shape[0]
    pad = LANES - dm.qk_nope - dm.qk_rope
    t1 = jnp.concatenate([jnp.ones((n, dm.qk_nope), F32), cos, cos, jnp.zeros((n, pad), F32)], axis=1)
    t2 = jnp.concatenate([jnp.zeros((n, dm.qk_nope), F32), -sin, sin, jnp.zeros((n, pad), F32)], axis=1)
    del half
    return t1, t2


def _mla_weights(w_in, w_uq, w_ukv, w_o, dm):
    hn, nope, rope, vh = dm.n_heads, dm.qk_nope, dm.qk_rope, dm.v_head
    half = rope // 2
    d = w_in.shape[0]
    lat = dm.q_lora + dm.kv_lora
    kpe = w_in[:, lat:]
    kpe_sw = jnp.concatenate([kpe[:, half:], kpe[:, :half]], axis=1)
    win = jnp.concatenate([w_in[:, :lat], jnp.zeros((d, nope), F32), kpe, kpe_sw], axis=1).astype(BF16)
    uq = w_uq.reshape(dm.q_lora, hn, nope + rope)
    wuq = jnp.concatenate([uq, uq[:, :, nope + half:], uq[:, :, nope:nope + half]], axis=2)
    wuq = wuq.reshape(dm.q_lora, hn * LANES).astype(BF16)
    ukv = w_ukv.reshape(dm.kv_lora, hn, nope + vh)
    wuk = jnp.concatenate([ukv[:, :, :nope], jnp.zeros((dm.kv_lora, hn, LANES - nope), F32)], axis=2)
    wuk = wuk.reshape(dm.kv_lora, hn * LANES).astype(BF16)
    wuv = jnp.concatenate([ukv[:, :, nope:].transpose(1, 2, 0), jnp.zeros((hn, VT_ROWS - vh, dm.kv_lora), F32)],
                          axis=1).reshape(hn * VT_ROWS, dm.kv_lora).astype(BF16)
    vone = jnp.tile((jnp.arange(VT_ROWS) == vh).astype(F32), hn)[:, None]
    wuk_t = jnp.concatenate([ukv[:, :, :nope].transpose(1, 2, 0),
                             jnp.zeros((hn, LANES - nope, dm.kv_lora), F32)], axis=1).astype(BF16)
    wuv_p = jnp.concatenate([ukv[:, :, nope:].transpose(1, 0, 2),
                             jnp.zeros((hn, dm.kv_lora, LANES - vh), F32)], axis=2).astype(BF16)
    wo_p = jnp.concatenate([w_o.reshape(hn, vh, d), jnp.zeros((hn, LANES - vh, d), F32)], axis=1).astype(BF16)
    return dict(win=win, wuq=wuq, wuk=wuk, wuv=wuv, vone=vone, wuk_t=wuk_t, wuv_p=wuv_p, wo_p=wo_p, wo=w_o.astype(BF16))


def _attn_prompt_body(q_ref, k_ref, vt_ref, o_ref, sa_ref, sb_ref, m_ref, acc_ref, *, tq, v_head):
    qi = pl.program_id(2)
    n_blocks = pl.num_programs(2)
    n_hh = q_ref.shape[-1] // LANES
    key = lax.broadcasted_iota(jnp.int32, (tq, tq), 0)
    qry = lax.broadcasted_iota(jnp.int32, (tq, tq), 1)
    nt = (((1,), (1,)), ((), ()))
    qh = [q_ref[:, hh * LANES:(hh + 1) * LANES] for hh in range(n_hh)]

    def scores_into(dst_ref, ki):
        r = pl.multiple_of(ki * tq, tq)
        for hh in range(n_hh):
            kk = k_ref[pl.ds(r, tq), hh * LANES:(hh + 1) * LANES]
            dst_ref[hh] = lax.dot_general(kk, qh[hh], nt, preferred_element_type=F32)

    def accumulate(src_ref, ki, masked):
        for hh in range(n_hh):
            st = src_ref[hh]
            if masked:
                st = jnp.where(key + ki * tq <= qry + qi * tq, st, -jnp.inf)
            m = m_ref[hh]
            m_new = jnp.maximum(m, jnp.max(st, axis=0, keepdims=True))
            alpha = jnp.exp2(m - m_new)
            pt = jnp.exp2(st - m_new).astype(BF16)
            m_ref[hh] = m_new
            vt = vt_ref[ki, hh * VT_ROWS:(hh + 1) * VT_ROWS, :]
            acc_ref[hh] = alpha * acc_ref[hh] + jnp.dot(vt, pt, preferred_element_type=F32)

    m_ref[...] = jnp.full(m_ref.shape, -jnp.inf, F32)
    acc_ref[...] = jnp.zeros(acc_ref.shape, F32)

    scores_into(sa_ref, 0)

    def pair(kp, carry):
        scores_into(sb_ref, 2 * kp + 1)
        accumulate(sa_ref, 2 * kp, False)
        scores_into(sa_ref, 2 * kp + 2)
        accumulate(sb_ref, 2 * kp + 1, False)
        return carry

    lax.fori_loop(0, qi // 2, pair, 0)
    k0 = 2 * (qi // 2)
    scores_into(sb_ref, jnp.minimum(k0 + 1, n_blocks - 1))
    accumulate(sa_ref, k0, True)

    @pl.when(qi % 2 == 1)
    def _():
        accumulate(sb_ref, k0 + 1, True)

    ot = jnp.concatenate([acc_ref[hh, 0:v_head, :] / acc_ref[hh, v_head:v_head + 1, :] for hh in range(n_hh)],
                         axis=0)
    o_ref[...] = ot.T.astype(o_ref.dtype)


def _attn_prompt(q, k, vt, batch, seq, dm):
    tq = vt.shape[-1]
    nq = seq // tq
    hp = dm.n_heads // 2
    k3 = k.reshape(batch, seq, -1)
    vt4 = vt.reshape(batch, nq, vt.shape[1], tq)
    body = functools.partial(_attn_prompt_body, tq=tq, v_head=dm.v_head)
    return pl.pallas_call(
        body,
        grid=(batch, hp, nq),
        in_specs=[pl.BlockSpec((tq, 2 * LANES), lambda b, h, i: (b * nq + i, h)),
                  pl.BlockSpec((None, seq, 2 * LANES), lambda b, h, i: (b, 0, h)),
                  pl.BlockSpec((None, nq, 2 * VT_ROWS, tq), lambda b, h, i: (b, 0, h, 0))],
        out_specs=pl.BlockSpec((tq, 2 * dm.v_head), lambda b, h, i: (b * nq + i, h)),
        out_shape=jax.ShapeDtypeStruct((batch * seq, dm.n_heads * dm.v_head), BF16),
        scratch_shapes=[pltpu.VMEM((2, tq, tq), F32), pltpu.VMEM((2, tq, tq), F32),
                        pltpu.VMEM((2, 1, tq), F32), pltpu.VMEM((2, VT_ROWS, tq), F32)],
        compiler_params=_params("arbitrary", "arbitrary", "arbitrary"),
        name="attn_prompt",
    )(q, k3, vt4)


def _attn_out_body(x_ref, o_ref, wo_ref, gpost_ref, y_ref):
    h = jnp.dot(o_ref[...], wo_ref[...], preferred_element_type=F32)
    y_ref[...] = x_ref[...] + _rms(h, gpost_ref[...])


def _attn_out(x, o, wo, gpost, dm):
    n, d = x.shape
    tm = min(dm.tm, n)
    return pl.pallas_call(
        _attn_out_body,
        grid=(n // tm,),
        in_specs=[pl.BlockSpec((tm, d), lambda i: (i, 0)), pl.BlockSpec((tm, o.shape[1]), lambda i: (i, 0)),
                  _const_spec(wo.shape), _const_spec((1, d))],
        out_specs=pl.BlockSpec((tm, d), lambda i: (i, 0)),
        out_shape=jax.ShapeDtypeStruct((n, d), F32),
        compiler_params=_params("arbitrary"),
        name="attn_out",
    )(x, o, wo, gpost)


def _sample_q_body(q_ref, wukt_ref, ql_ref, qr_ref, *, n_heads, qk_nope, seq):
    n = q_ref.shape[0]
    lane = lax.broadcasted_iota(jnp.int32, (LANES, LANES), 1)
    src = lax.broadcasted_iota(jnp.int32, (LANES, LANES), 0)
    sel = (src == lane + qk_nope).astype(BF16)
    for hd in range(n_heads):
        qh = q_ref[:, hd * LANES:(hd + 1) * LANES]
        ql = jnp.dot(qh, wukt_ref[hd], preferred_element_type=F32)
        qr = jnp.dot(qh, sel, preferred_element_type=F32)
        ql_ref[:, hd] = ql.reshape(n // seq, seq, ql.shape[-1])
        qr_ref[:, hd] = qr.reshape(n // seq, seq, LANES)


def _sample_q(q, wuk_t, dec_batch, dec_seq, dm):
    c = dm.kv_lora
    body = functools.partial(_sample_q_body, n_heads=dm.n_heads, qk_nope=dm.qk_nope, seq=dec_seq)
    return pl.pallas_call(
        body,
        out_shape=[jax.ShapeDtypeStruct((dec_batch, dm.n_heads, dec_seq, c), F32),
                   jax.ShapeDtypeStruct((dec_batch, dm.n_heads, dec_seq, LANES), F32)],
        compiler_params=_params(),
        name="sample_q",
    )(q, wuk_t)


def _sample_attn_body(pt_ref, ql_ref, qr_ref, ckvn_ref, kpent_ref, ckv_hbm, kpet_hbm, o_ref,
                      m_ref, l_ref, acc_ref, cat_ref, kp_ref, kn_ref, ckv_buf, kpe_buf, sem, *, n_pages, seq):
    b = pl.program_id(0)
    j = pl.program_id(1)
    n_b = pl.num_programs(0)
    n_j = pl.num_programs(1)
    step = b * n_j + j
    slot = step % 2
    rows, c = acc_ref.shape
    page = ckv_buf.shape[2]
    rope = kpe_buf.shape[2]

    def page_copies(bb, jj, sl, g):
        pg = pt_ref[bb, jj * n_pages + g]
        return (pltpu.make_async_copy(ckv_hbm.at[pg], ckv_buf.at[sl, g], sem.at[sl, 0]),
                pltpu.make_async_copy(kpet_hbm.at[pg], kpe_buf.at[sl, g], sem.at[sl, 1]))

    def start_fetch(bb, jj, sl):
        for g in range(n_pages):
            for cp in page_copies(bb, jj, sl, g):
                cp.start()

    def wait_fetch(bb, jj, sl):
        for g in range(n_pages):
            for cp in page_copies(bb, jj, sl, g):
                cp.wait()

    @pl.when(step == 0)
    def _():
        start_fetch(0, 0, 0)

    last_j = j + 1 == n_j
    j_next = jnp.where(last_j, 0, j + 1)
    b_next = jnp.where(last_j, jnp.where(b + 1 == n_b, 0, b + 1), b)
    start_fetch(b_next, j_next, 1 - slot)

    ql = ql_ref[0].reshape(rows, c).astype(BF16)
    qr = qr_ref[0].reshape(rows, LANES).astype(BF16)
    nt = (((1,), (1,)), ((), ()))

    @pl.when(j == 0)
    def _():
        kp_ref[...] = jnp.zeros_like(kp_ref)
        kn_ref[...] = jnp.zeros_like(kn_ref)
        kn_ref[0:rope, 0:seq] = kpent_ref[0]
        cn = jnp.concatenate([ckvn_ref[0], jnp.zeros((page - seq, c), F32)], axis=0).astype(BF16)
        s = (lax.dot_general(ql, cn, nt, preferred_element_type=F32)
             + jnp.dot(qr, kn_ref[...].astype(BF16), preferred_element_type=F32))
        r = lax.broadcasted_iota(jnp.int32, s.shape, 0)
        k = lax.broadcasted_iota(jnp.int32, s.shape, 1)
        s = jnp.where(k <= r % seq, s, -jnp.inf)
        m = jnp.max(s, axis=-1, keepdims=True)
        p = jnp.exp2(s - m)
        m_ref[...] = m
        l_ref[...] = jnp.sum(p, axis=-1, keepdims=True)
        acc_ref[...] = jnp.dot(p.astype(BF16), cn, preferred_element_type=F32)

    wait_fetch(b, j, slot)
    n_chunks = 4 if n_pages % 4 == 0 else 1
    cp = n_pages // n_chunks * page
    s = []
    for ch in range(n_chunks):
        for g in range(ch * n_pages // n_chunks, (ch + 1) * n_pages // n_chunks):
            cat_ref[g * page:(g + 1) * page, :] = ckv_buf[slot, g].astype(BF16)
            kp_ref[0:rope, g * page:(g + 1) * page] = kpe_buf[slot, g].astype(BF16)
        s.append(lax.dot_general(ql, cat_ref[ch * cp:(ch + 1) * cp, :], nt, preferred_element_type=F32)
                 + jnp.dot(qr, kp_ref[:, ch * cp:(ch + 1) * cp], preferred_element_type=F32))
    m = m_ref[...]
    m_new = m
    for sc in s:
        m_new = jnp.maximum(m_new, jnp.max(sc, axis=-1, keepdims=True))
    alpha = jnp.exp2(m - m_new)
    l = alpha * l_ref[...]
    acc = alpha * acc_ref[...]
    for ch, sc in enumerate(s):
        p = jnp.exp2(sc - m_new)
        l = l + jnp.sum(p, axis=-1, keepdims=True)
        acc = acc + jnp.dot(p.astype(BF16), cat_ref[ch * cp:(ch + 1) * cp, :], preferred_element_type=F32)
    m_ref[...] = m_new
    l_ref[...] = l
    acc_ref[...] = acc

    @pl.when(last_j)
    def _():
        o_ref[0] = acc_ref[...] / l_ref[...]

    @pl.when(step == n_b * n_j - 1)
    def _():
        wait_fetch(b_next, j_next, 1 - slot)


def _sample_attn(q_lat, q_rope, ckv_new, kpe_new_t, cache_ckv, cache_kpe_t, page_table, dm):
    bd, hn, seq, c = q_lat.shape
    n_log = page_table.shape[1]
    pps = min(dm.pages_per_step, n_log)
    page = cache_ckv.shape[1]
    rope = cache_kpe_t.shape[1]
    rows = hn * seq
    body = functools.partial(_sample_attn_body, n_pages=pps, seq=seq)
    grid_spec = pltpu.PrefetchScalarGridSpec(
        num_scalar_prefetch=1,
        grid=(bd, n_log // pps),
        in_specs=[pl.BlockSpec((1, hn, seq, c), lambda b, j, pt: (b, 0, 0, 0)),
                  pl.BlockSpec((1, hn, seq, LANES), lambda b, j, pt: (b, 0, 0, 0)),
                  pl.BlockSpec((1, seq, c), lambda b, j, pt: (b, 0, 0)),
                  pl.BlockSpec((1, rope, seq), lambda b, j, pt: (b, 0, 0)),
                  pl.BlockSpec(memory_space=pl.ANY), pl.BlockSpec(memory_space=pl.ANY)],
        out_specs=pl.BlockSpec((1, rows, c), lambda b, j, pt: (b, 0, 0)),
        scratch_shapes=[pltpu.VMEM((rows, 1), F32), pltpu.VMEM((rows, 1), F32), pltpu.VMEM((rows, c), F32),
                        pltpu.VMEM((pps * page, c), BF16), pltpu.VMEM((LANES, pps * page), BF16),
                        pltpu.VMEM((LANES, page), F32),
                        pltpu.VMEM((2, pps, page, c), F32), pltpu.VMEM((2, pps, rope, page), F32),
                        pltpu.SemaphoreType.DMA((2, 2))],
    )
    return pl.pallas_call(
        body,
        grid_spec=grid_spec,
        out_shape=jax.ShapeDtypeStruct((bd, rows, c), F32),
        compiler_params=_params("arbitrary", "arbitrary"),
        name="attn_sample",
    )(page_table, q_lat, q_rope, ckv_new, kpe_new_t, cache_ckv, cache_kpe_t)


def _sample_out_body(x_ref, ol_ref, wuv_ref, wo_ref, gpost_ref, y_ref, *, n_heads):
    bb, _, seq, c = ol_ref.shape
    h = None
    for hd in range(n_heads):
        ol = ol_ref[:, hd].reshape(bb * seq, c).astype(BF16)
        oh = jnp.dot(ol, wuv_ref[hd], preferred_element_type=F32).astype(BF16)
        part = jnp.dot(oh, wo_ref[hd], preferred_element_type=F32)
        h = part if h is None else h + part
    y_ref[...] = x_ref[...] + _rms(h, gpost_ref[...])


def _sample_out(x, o_lat, wuv_p, wo_p, gpost, dm):
    n, d = x.shape
    bd, hn, seq, c = o_lat.shape
    bb = min(dm.out_bb, bd)
    body = functools.partial(_sample_out_body, n_heads=hn)
    return pl.pallas_call(
        body,
        grid=(bd // bb,),
        in_specs=[pl.BlockSpec((bb * seq, d), lambda i: (i, 0)),
                  pl.BlockSpec((bb, hn, seq, c), lambda i: (i, 0, 0, 0)),
                  _const_spec(wuv_p.shape), _const_spec(wo_p.shape), _const_spec((1, d))],
        out_specs=pl.BlockSpec((bb * seq, d), lambda i: (i, 0)),
        out_shape=jax.ShapeDtypeStruct((n, d), F32),
        compiler_params=_params("arbitrary"),
        name="sample_out",
    )(x, o_lat, wuv_p, wo_p, gpost)


def _forward(x, pos_tab, h0_re, h0_im, attend, w, dm, *, s5_tc, s5_bb):
    b, t, d = x.shape
    n = b * t
    xf = x.reshape(n, d)
    g = lambda a, i, k: a[i, k][None, :]
    xf = _ffn(xf, g(w['norm_pre'], 0, 0), g(w['norm_post'], 0, 0), *w['ffn'][0][0], dm)
    y3, hfin = _s5(xf.reshape(b, t, d), _state_to_slabs(h0_re, h0_im), g(w['norm_pre'], 0, 1),
                   g(w['norm_post'], 0, 1), w['bblk'], w['cw'], w['a_re'], w['a_im'], w['ssm_d'], w['wglu'], dm,
                   tc=s5_tc, bb=s5_bb)
    xf = y3.reshape(n, d)
    s_re, s_im = _slabs_to_state(hfin, d // dm.ssm_group, dm.ssm_state)
    xf = _ffn(xf, g(w['norm_pre'], 0, 2), g(w['norm_post'], 0, 2), *w['ffn'][0][1], dm)
    xf = _ffn(xf, g(w['norm_pre'], 1, 0), g(w['norm_post'], 1, 0), *w['ffn'][1][0], dm)
    xf, c_kv, k_pe = attend(xf, pos_tab, g(w['norm_pre'], 1, 1), g(w['norm_post'], 1, 1))
    xf = _ffn(xf, g(w['norm_pre'], 1, 2), g(w['norm_post'], 1, 2), *w['ffn'][1][1], dm)
    return (xf.reshape(b, t, d), s_re[None], s_im[None],
            c_kv.reshape(1, b, t, -1), k_pe.reshape(1, b, t, -1))


def _run(dm, x_prompt, x_sample, state_ssm_re, state_ssm_im, cache_kv_latent, cache_k_rope, page_table,
         norm_pre, norm_post, ffn_w_gate, ffn_w_up, ffn_w_down,
         ssm_a_re, ssm_a_im, ssm_log_dt, ssm_b_re, ssm_b_im, ssm_c_re, ssm_c_im, ssm_d, ssm_w_glu,
         mla_w_in, mla_q_norm, mla_kv_norm, mla_w_uq, mla_w_ukv, mla_w_o):
    b_p, s_p, d = x_prompt.shape
    b_s, s_s, _ = x_sample.shape
    past_len = page_table.shape[1] * dm.page_size
    depth = norm_pre.shape[0]
    assert depth == 2 and ssm_a_re.shape[0] == 1 and mla_w_in.shape[0] == 1

    ab_re, ab_im, bb_re, bb_im = _ssm_prep(ssm_a_re[0], ssm_a_im[0], ssm_log_dt[0], ssm_b_re[0], ssm_b_im[0])
    bblk, cw, a_re, a_im = _s5_weights(bb_re, bb_im, ssm_c_re[0], ssm_c_im[0], ab_re, ab_im, dm)
    mw = _mla_weights(mla_w_in[0], mla_w_uq[0], mla_w_ukv[0], mla_w_o[0], dm)
    w = dict(norm_pre=norm_pre, norm_post=norm_post,
             ffn=[[(ffn_w_gate[i, k].astype(BF16), ffn_w_up[i, k].astype(BF16), ffn_w_down[i, k].astype(BF16))
                   for k in range(2)] for i in range(depth)],
             bblk=bblk, cw=cw, a_re=a_re, a_im=a_im, ssm_d=ssm_d[0][None, :], wglu=ssm_w_glu[0].astype(BF16))
    qn = mla_q_norm[0][None, :]
    kvn = mla_kv_norm[0][None, :]

    def prompt_attend(xf, tabs, gpre, gpost):
        q, c_kv, k_pe, kmat, vt = _mla_proj(xf, gpre, mw['win'], qn, kvn, mw['wuq'], mw['wuk'], mw['wuv'],
                                            mw['vone'], *tabs, dm, with_kv=True)
        o = _attn_prompt(q, kmat, vt, b_p, s_p, dm)
        return _attn_out(xf, o, mw['wo'], gpost, dm), c_kv, k_pe

    def sample_attend(xf, tabs, gpre, gpost):
        q, c_kv, k_pe = _mla_proj(xf, gpre, mw['win'], qn, kvn, mw['wuq'], mw['wuk'], mw['wuv'],
                                  mw['vone'], *tabs, dm, with_kv=False)
        q_lat, q_rope = _sample_q(q, mw['wuk_t'], b_s, s_s, dm)
        o_lat = _sample_attn(q_lat, q_rope, c_kv.reshape(b_s, s_s, -1), k_pe.reshape(b_s, s_s, -1).swapaxes(1, 2),
                             cache_kv_latent[0], cache_k_rope[0].swapaxes(1, 2), page_table, dm)
        o_lat = o_lat.reshape(b_s, dm.n_heads, s_s, -1)
        return _sample_out(xf, o_lat, mw['wuv_p'], mw['wo_p'], gpost, dm), c_kv, k_pe

    tm_p = min(dm.tm, b_p * s_p)
    tabs_p = _rope_tables(jnp.arange(max(s_p, tm_p), dtype=jnp.int32) % s_p, dm)
    h0 = jnp.zeros((b_p,) + state_ssm_re.shape[2:], F32)
    tc_p = max(dm.s5_rows // b_p, 1)
    out_p = _forward(x_prompt, tabs_p, h0, h0, prompt_attend, w, dm, s5_tc=min(tc_p, s_p), s5_bb=b_p)
    tm_s = min(dm.tm, b_s * s_s)
    tabs_s = _rope_tables(past_len + jnp.arange(max(s_s, tm_s), dtype=jnp.int32) % s_s, dm)
    bb_s = min(max(dm.s5_rows // s_s, SUBLANES), b_s)
    out_s = _forward(x_sample, tabs_s, state_ssm_re[0], state_ssm_im[0], sample_attend, w, dm,
                     s5_tc=s_s, s5_bb=bb_s)
    y_p, re_p, im_p, lat_p, kpe_p = out_p
    y_s, re_s, im_s, lat_s, kpe_s = out_s
    return (y_p, y_s, re_p, im_p, re_s, im_s, lat_p, kpe_p, lat_s, kpe_s)


def kernel(x_prompt, x_sample, state_ssm_re, state_ssm_im, cache_kv_latent, cache_k_rope, page_table,
           norm_pre, norm_post, ffn_w_gate, ffn_w_up, ffn_w_down,
           ssm_a_re, ssm_a_im, ssm_log_dt, ssm_b_re, ssm_b_im, ssm_c_re, ssm_c_im, ssm_d, ssm_w_glu,
           mla_w_in, mla_q_norm, mla_kv_norm, mla_w_uq, mla_w_ukv, mla_w_o):
    return _run(DIMS, x_prompt, x_sample, state_ssm_re, state_ssm_im, cache_kv_latent, cache_k_rope, page_table,
                norm_pre, norm_post, ffn_w_gate, ffn_w_up, ffn_w_down,
                ssm_a_re, ssm_a_im, ssm_log_dt, ssm_b_re, ssm_b_im, ssm_c_re, ssm_c_im, ssm_d, ssm_w_glu,
                mla_w_in, mla_q_norm, mla_kv_norm, mla_w_uq, mla_w_ukv, mla_w_o)
```

```python
import functools
import math
from typing import NamedTuple

import jax
import jax.numpy as jnp
from jax import lax
from jax.experimental import pallas as pl
from jax.experimental.pallas import tpu as pltpu

F32 = jnp.float32
BF16 = jnp.bfloat16

LANES = 128
VT_ROWS = 80
SUBLANES = 8
MXU_DIM = 256
VMEM_LIMIT_BYTES = 56 * 1024 * 1024

EPS = 1e-6
ROPE_THETA = 10000.0


class Dims(NamedTuple):
    d_model: int = 1024
    d_ff: int = 2816
    ssm_group: int = 16
    ssm_state: int = 64
    n_heads: int = 16
    qk_nope: int = 64
    qk_rope: int = 32
    v_head: int = 64
    kv_lora: int = 256
    q_lora: int = 768
    page_size: int = 128
    tm: int = 512
    ffn_chunk: int = 1536
    ffn_row_split: int = 2
    s5_rows: int = 256
    pages_per_step: int = 16
    seqs_per_step: int = 2
    out_bb: int = 32


DIMS = Dims()


def _rms(x, g):
    ms = jnp.mean(x * x, axis=-1, keepdims=True)
    return x * lax.rsqrt(ms + EPS) * g


def _const_spec(shape):
    zeros = (0,) * len(shape)
    return pl.BlockSpec(shape, lambda *_: zeros, pipeline_mode=pl.Buffered(1))


def _params(*sem):
    return pltpu.CompilerParams(dimension_semantics=sem, vmem_limit_bytes=VMEM_LIMIT_BYTES)


def _ffn_body(x_ref, gpre_ref, gpost_ref, wg_ref, wu_ref, wd_ref, o_ref, *, d_ff, chunk, row_split):
    rows = x_ref.shape[0] // row_split
    for r in range(row_split):
        rs = slice(r * rows, (r + 1) * rows)
        x = x_ref[rs, :]
        h = _rms(x, gpre_ref[...]).astype(BF16)
        y = None
        for c0 in range(0, d_ff, chunk):
            sl = slice(c0, min(c0 + chunk, d_ff))
            g = jnp.dot(h, wg_ref[:, sl], preferred_element_type=F32)
            u = jnp.dot(h, wu_ref[:, sl], preferred_element_type=F32)
            a = (g * jax.nn.sigmoid(g) * u).astype(BF16)
            part = jnp.dot(a, wd_ref[sl, :], preferred_element_type=F32)
            y = part if y is None else y + part
        o_ref[rs, :] = x + 0.5 * _rms(y, gpost_ref[...])


def _ffn(x, gpre, gpost, wg, wu, wd, layer, which, dm):
    n, d = x.shape
    tm = min(dm.tm, n)
    body = functools.partial(_ffn_body, d_ff=dm.d_ff, chunk=dm.ffn_chunk, row_split=dm.ffn_row_split)

    def weight(w):
        return pl.BlockSpec((None, None) + w.shape[2:], lambda i: (layer, which, 0, 0), pipeline_mode=pl.Buffered(1))

    return pl.pallas_call(
        body,
        grid=(n // tm,),
        in_specs=[pl.BlockSpec((tm, d), lambda i: (i, 0)),
                  _const_spec((1, d)), _const_spec((1, d)), weight(wg), weight(wu), weight(wd)],
        out_specs=pl.BlockSpec((tm, d), lambda i: (i, 0)),
        out_shape=jax.ShapeDtypeStruct((n, d), F32),
        compiler_params=_params("arbitrary"),
        name="ffn",
    )(x, gpre, gpost, wg, wu, wd)


def _ssm_prep_body(are_ref, aim_ref, ldt_ref, bre_ref, bim_ref, abre_ref, abim_ref, bbre_ref, bbim_ref):
    dt = jnp.exp(ldt_ref[...])
    lr = are_ref[...]
    li = aim_ref[...]
    mag = jnp.exp(lr * dt)
    ang = li * dt
    ab_re = mag * jnp.cos(ang)
    ab_im = mag * jnp.sin(ang)
    nr = ab_re - 1.0
    den = lr * lr + li * li
    g_re = (nr * lr + ab_im * li) / den
    g_im = (ab_im * lr - nr * li) / den
    abre_ref[...] = ab_re
    abim_ref[...] = ab_im
    b_re = bre_ref[...]
    b_im = bim_ref[...]
    bbre_ref[...] = g_re * b_re - g_im * b_im
    bbim_ref[...] = g_re * b_im + g_im * b_re


def _ssm_prep(a_re, a_im, log_dt, b_re, b_im):
    g, p = a_re.shape
    i = b_re.shape[-1]
    n = g * p
    col = lambda a: a.reshape(n, 1)
    ldt = jnp.broadcast_to(log_dt[:, None], (g, p))
    shp = [jax.ShapeDtypeStruct((n, 1), F32)] * 2 + [jax.ShapeDtypeStruct((n, i), F32)] * 2
    ab_re, ab_im, bb_re, bb_im = pl.pallas_call(_ssm_prep_body, out_shape=shp, name="ssm_prep")(
        col(a_re), col(a_im), col(ldt), b_re.reshape(n, i), b_im.reshape(n, i))
    return ab_re.reshape(g, p), ab_im.reshape(g, p), bb_re.reshape(g, p, i), bb_im.reshape(g, p, i)


def _s5_body(x_ref, h0_ref, perm_ref, permt_ref, gpre_ref, gpost_ref, bblk_ref, cw_ref, are_ref, aim_ref, d_ref,
             wglu_ref, o_ref, hfin_ref, s_ref, hst_ref, *, tc, bb, n_ktile, slabs_per_ktile, slabs_per_out):
    ti = pl.program_id(1)
    rows = tc * bb
    d = x_ref.shape[-1]
    n_slab = s_ref.shape[0]
    x = x_ref[...].reshape(rows, d)
    u_bt = _rms(x, gpre_ref[...])
    hi = u_bt.astype(BF16)
    rest = u_bt - hi.astype(F32)
    mid = rest.astype(BF16)
    lo = (rest - mid.astype(F32)).astype(BF16)
    perm = perm_ref[...]
    ub_f32 = jnp.dot(perm, hi, preferred_element_type=F32)
    ub = ub_f32.astype(BF16)
    u = ub_f32 + jnp.dot(perm, mid, preferred_element_type=F32) + jnp.dot(perm, lo, preferred_element_type=F32)
    for j in range(n_ktile):
        bu = jnp.dot(ub[:, j * MXU_DIM:(j + 1) * MXU_DIM], bblk_ref[j], preferred_element_type=F32)
        for q in range(slabs_per_ktile):
            s_ref[j * slabs_per_ktile + q] = bu[:, q * 2 * LANES:(q + 1) * 2 * LANES]

    @pl.when(ti == 0)
    def _():
        hst_ref[...] = h0_ref[...]

    n_par = 4
    n_sub = bb // SUBLANES

    def slab_group(kg, carry):
        ks = [kg * n_par + q for q in range(n_par)]
        ar = [jnp.broadcast_to(are_ref[k], (SUBLANES, LANES)) for k in ks]
        ai = [jnp.broadcast_to(aim_ref[k], (SUBLANES, LANES)) for k in ks]

        def sub_group(sg, carry2):
            r0 = pl.multiple_of(sg * SUBLANES, SUBLANES)
            hr = tuple(hst_ref[k, pl.ds(r0, SUBLANES), 0:LANES] for k in ks)
            hi = tuple(hst_ref[k, pl.ds(r0, SUBLANES), LANES:2 * LANES] for k in ks)

            def step(t, h):
                hr, hi = h
                r = pl.multiple_of(t * bb + r0, SUBLANES)
                nr, ni = [], []
                for q, k in enumerate(ks):
                    xr = s_ref[k, pl.ds(r, SUBLANES), 0:LANES]
                    xi = s_ref[k, pl.ds(r, SUBLANES), LANES:2 * LANES]
                    vr = ar[q] * hr[q] - ai[q] * hi[q] + xr
                    vi = ar[q] * hi[q] + ai[q] * hr[q] + xi
                    s_ref[k, pl.ds(r, SUBLANES), 0:LANES] = vr
                    s_ref[k, pl.ds(r, SUBLANES), LANES:2 * LANES] = vi
                    nr.append(vr)
                    ni.append(vi)
                return tuple(nr), tuple(ni)

            hr, hi = lax.fori_loop(0, tc, step, (hr, hi), unroll=min(tc, 8))
            for q, k in enumerate(ks):
                hst_ref[k, pl.ds(r0, SUBLANES), 0:LANES] = hr[q]
                hst_ref[k, pl.ds(r0, SUBLANES), LANES:2 * LANES] = hi[q]
            return carry2

        lax.fori_loop(0, n_sub, sub_group, 0)
        return carry

    lax.fori_loop(0, n_slab // n_par, slab_group, 0)
    hfin_ref[...] = hst_ref[...]

    pieces = []
    for j in range(n_slab // slabs_per_out):
        acc = None
        for q in range(slabs_per_out):
            k = j * slabs_per_out + q
            part = jnp.dot(s_ref[k].astype(BF16), cw_ref[k], preferred_element_type=F32)
            acc = part if acc is None else acc + part
        pieces.append(acc)
    y = jnp.concatenate(pieces, axis=-1) + d_ref[...] * u
    hg_tb = jax.nn.gelu(y).astype(BF16)
    hg = jnp.dot(permt_ref[...], hg_tb, preferred_element_type=F32).astype(BF16)
    z = jnp.dot(hg, wglu_ref[...], preferred_element_type=F32)
    mix = z[:, :d] * jax.nn.sigmoid(z[:, d:])
    o_ref[...] = (x + _rms(mix, gpost_ref[...])).reshape(bb, tc, d)


def _s5(x, h0, gpre, gpost, bblk, cw, a_re, a_im, dvec, wglu, dm, *, tc, bb):
    b, t, d = x.shape
    n_slab = cw.shape[0]
    n_ktile = bblk.shape[0]
    rows = tc * bb
    out_lanes_per_slab = (LANES // dm.ssm_state) * dm.ssm_group
    body = functools.partial(_s5_body, tc=tc, bb=bb, n_ktile=n_ktile, slabs_per_ktile=n_slab // n_ktile,
                             slabs_per_out=LANES // out_lanes_per_slab)
    r_out = jnp.arange(rows)
    perm = jax.nn.one_hot((r_out % bb) * tc + r_out // bb, rows, dtype=BF16)
    return pl.pallas_call(
        body,
        grid=(b // bb, t // tc),
        in_specs=[pl.BlockSpec((bb, tc, d), lambda bi, ti: (bi, ti, 0)),
                  pl.BlockSpec((n_slab, bb, 2 * LANES), lambda bi, ti: (0, bi, 0)),
                  _const_spec((rows, rows)), _const_spec((rows, rows)),
                  _const_spec((1, d)), _const_spec((1, d)),
                  _const_spec(bblk.shape), _const_spec(cw.shape),
                  _const_spec(a_re.shape), _const_spec(a_im.shape),
                  _const_spec((1, d)), _const_spec(wglu.shape)],
        out_specs=[pl.BlockSpec((bb, tc, d), lambda bi, ti: (bi, ti, 0)),
                   pl.BlockSpec((n_slab, bb, 2 * LANES), lambda bi, ti: (0, bi, 0))],
        out_shape=[jax.ShapeDtypeStruct((b, t, d), F32),
                   jax.ShapeDtypeStruct((n_slab, b, 2 * LANES), F32)],
        scratch_shapes=[pltpu.VMEM((n_slab, rows, 2 * LANES), F32),
                        pltpu.VMEM((n_slab, bb, 2 * LANES), F32)],
        compiler_params=_params("arbitrary", "arbitrary"),
        name="s5_mixer",
    )(x, h0, perm, perm.T, gpre, gpost, bblk, cw, a_re, a_im, dvec, wglu)


def _s5_weights(bb_re, bb_im, c_re, c_im, ab_re, ab_im, dm):
    g, p, i = bb_re.shape
    gps = LANES // p
    n_slab = g // gps
    g_tile = MXU_DIM // i
    n_ktile = g // g_tile
    eye_t = jnp.eye(g_tile, dtype=F32)

    def b_tiles(bm):
        bt = bm.transpose(0, 2, 1).reshape(n_ktile, g_tile, i, p)
        return jnp.einsum('jgip,gh->jgihp', bt, eye_t).reshape(n_ktile, g_tile * i, g_tile // gps, LANES)

    bblk = jnp.stack([b_tiles(bb_re), b_tiles(bb_im)], axis=3)
    bblk = bblk.reshape(n_ktile, g_tile * i, g_tile * p * 2).astype(BF16)

    spo = LANES // (gps * i)
    eye_g = jnp.eye(gps, dtype=F32)
    sel = jax.nn.one_hot(jnp.arange(n_slab) % spo, spo, dtype=F32)

    def c_slabs(cm):
        ct = cm.reshape(n_slab, gps, i, p)
        base = jnp.einsum('khip,hg->khpgi', ct, eye_g).reshape(n_slab, gps * p, gps * i)
        return jnp.einsum('krc,kq->krqc', base, sel).reshape(n_slab, LANES, LANES)

    cw = jnp.concatenate([c_slabs(c_re), -c_slabs(c_im)], axis=1).astype(BF16)
    a_re = ab_re.reshape(n_slab, 1, LANES)
    a_im = ab_im.reshape(n_slab, 1, LANES)
    return bblk, cw, a_re, a_im


def _state_to_slabs(h_re, h_im):
    b = h_re.shape[0]
    re = h_re.reshape(b, -1, LANES)
    im = h_im.reshape(b, -1, LANES)
    return jnp.concatenate([re, im], axis=-1).transpose(1, 0, 2)


def _slabs_to_state(hs, g, p):
    n_slab, b, _ = hs.shape
    hs = hs.transpose(1, 0, 2)
    re = hs[:, :, :LANES].reshape(b, g, p)
    im = hs[:, :, LANES:].reshape(b, g, p)
    return re, im


def _q_scale(dm):
    return (dm.qk_nope + dm.qk_rope) ** -0.5 * math.log2(math.e)


def _rope_block(blk, t1, t2):
    return blk * t1 + pltpu.roll(blk, 3 * LANES // 4, 1) * t2


def _mla_proj_body(x_ref, gpre_ref, win_ref, qn_ref, kvn_ref, wuq_ref, wuk_ref, wuv_ref, vone_ref, t1_ref, t2_ref,
                   q_ref, ckv_ref, kpe_ref, *kv_refs, q_lora, kv_lora, n_heads, qk_rope, q_scale):
    x = x_ref[...]
    h = _rms(x, gpre_ref[...]).astype(BF16)
    hin = jnp.dot(h, win_ref[...], preferred_element_type=F32)
    cq = _rms(hin[:, :q_lora], qn_ref[...]).astype(BF16)
    ckv = _rms(hin[:, q_lora:q_lora + kv_lora], kvn_ref[...])
    t1 = t1_ref[...]
    t2 = t2_ref[...]
    kpe_blk = _rope_block(hin[:, q_lora + kv_lora:], t1, t2)
    ckv_ref[...] = ckv
    kpe_ref[...] = pltpu.roll(kpe_blk, LANES // 2, 1)[:, :qk_rope]
    q = jnp.dot(cq, wuq_ref[...], preferred_element_type=F32)
    for hd in range(n_heads):
        sl = slice(hd * LANES, (hd + 1) * LANES)
        q_ref[:, sl] = (_rope_block(q[:, sl], t1, t2) * q_scale).astype(BF16)
    if kv_refs:
        k_ref, v_ref = kv_refs
        cb = ckv.astype(BF16)
        kn = jnp.dot(cb, wuk_ref[...], preferred_element_type=F32)
        for hd in range(n_heads):
            sl = slice(hd * LANES, (hd + 1) * LANES)
            k_ref[:, sl] = (kn[:, sl] + kpe_blk).astype(BF16)
        vt = lax.dot_general(wuv_ref[...], cb, (((1,), (1,)), ((), ())), preferred_element_type=F32)
        v_ref[...] = (vt + vone_ref[...]).astype(BF16)


def _mla_proj(x, gpre, win, qn, kvn, wuq, wuk, wuv, vone, t1, t2, dm, *, with_kv):
    n, d = x.shape
    tm = min(dm.tm, n)
    n_tab = t1.shape[0] // tm
    hq = dm.n_heads * LANES
    hv = wuv.shape[0]
    body = functools.partial(_mla_proj_body, q_lora=dm.q_lora, kv_lora=dm.kv_lora, n_heads=dm.n_heads,
                             qk_rope=dm.qk_rope, q_scale=_q_scale(dm))
    row = lambda w: pl.BlockSpec((tm, w), lambda i: (i, 0))
    tab = pl.BlockSpec((tm, LANES), lambda i: (i % n_tab, 0))
    out_specs = [row(hq), row(dm.kv_lora), row(dm.qk_rope)]
    out_shape = [jax.ShapeDtypeStruct((n, hq), BF16), jax.ShapeDtypeStruct((n, dm.kv_lora), F32),
                 jax.ShapeDtypeStruct((n, dm.qk_rope), F32)]
    if with_kv:
        out_specs += [row(hq), pl.BlockSpec((None, hv, tm), lambda i: (i, 0, 0))]
        out_shape += [jax.ShapeDtypeStruct((n, hq), BF16), jax.ShapeDtypeStruct((n // tm, hv, tm), BF16)]
    return pl.pallas_call(
        body,
        grid=(n // tm,),
        in_specs=[row(d), _const_spec((1, d)), _const_spec(win.shape), _const_spec(qn.shape),
                  _const_spec(kvn.shape), _const_spec(wuq.shape), _const_spec(wuk.shape),
                  _const_spec(wuv.shape), _const_spec(vone.shape), tab, tab],
        out_specs=out_specs,
        out_shape=out_shape,
        compiler_params=_params("arbitrary"),
        name="mla_proj_kv" if with_kv else "mla_proj",
    )(x, gpre, win, qn, kvn, wuq, wuk, wuv, vone, t1, t2)


def _rope_tables(pos, dm):
    half = dm.qk_rope // 2
    inv = ROPE_THETA ** (-jnp.arange(0, dm.qk_rope, 2, dtype=F32) / dm.qk_rope)
    ang = pos.astype(F32)[:, None] * inv[None, :]
    cos, sin = jnp.cos(ang), jnp.sin(ang)
    n = pos.shape[0]
    pad = LANES - dm.qk_nope - dm.qk_rope
    t1 = jnp.concatenate([jnp.ones((n, dm.qk_nope), F32), cos, cos, jnp.zeros((n, pad), F32)], axis=1)
    t2 = jnp.concatenate([jnp.zeros((n, dm.qk_nope), F32), -sin, sin, jnp.zeros((n, pad), F32)], axis=1)
    del half
    return t1, t2


def _mla_weights(w_in, w_uq, w_ukv, w_o, dm):
    hn, nope, rope, vh = dm.n_heads, dm.qk_nope, dm.qk_rope, dm.v_head
    half = rope // 2
    d = w_in.shape[0]
    lat = dm.q_lora + dm.kv_lora
    kpe = w_in[:, lat:]
    kpe_sw = jnp.concatenate([kpe[:, half:], kpe[:, :half]], axis=1)
    win = jnp.concatenate([w_in[:, :lat], jnp.zeros((d, nope), F32), kpe, kpe_sw], axis=1).astype(BF16)
    uq = w_uq.reshape(dm.q_lora, hn, nope + rope)
    wuq = jnp.concatenate([uq, uq[:, :, nope + half:], uq[:, :, nope:nope + half]], axis=2)
    wuq = wuq.reshape(dm.q_lora, hn * LANES).astype(BF16)
    ukv = w_ukv.reshape(dm.kv_lora, hn, nope + vh)
    wuk = jnp.concatenate([ukv[:, :, :nope], jnp.zeros((dm.kv_lora, hn, LANES - nope), F32)], axis=2)
    wuk = wuk.reshape(dm.kv_lora, hn * LANES).astype(BF16)
    wuv = jnp.concatenate([ukv[:, :, nope:].transpose(1, 2, 0), jnp.zeros((hn, VT_ROWS - vh, dm.kv_lora), F32)],
                          axis=1).reshape(hn * VT_ROWS, dm.kv_lora).astype(BF16)
    vone = jnp.tile((jnp.arange(VT_ROWS) == vh).astype(F32), hn)[:, None]
    wuk_t = jnp.concatenate([ukv[:, :, :nope].transpose(1, 2, 0),
                             jnp.zeros((hn, LANES - nope, dm.kv_lora), F32)], axis=1).astype(BF16)
    wuv_p = jnp.concatenate([ukv[:, :, nope:].transpose(1, 0, 2),
                             jnp.zeros((hn, dm.kv_lora, LANES - vh), F32)], axis=2).astype(BF16)
    wo_p = jnp.concatenate([w_o.reshape(hn, vh, d), jnp.zeros((hn, LANES - vh, d), F32)], axis=1).astype(BF16)
    return dict(win=win, wuq=wuq, wuk=wuk, wuv=wuv, vone=vone, wuk_t=wuk_t, wuv_p=wuv_p, wo_p=wo_p, wo=w_o.astype(BF16))


def _attn_prompt_body(q_ref, k_ref, vt_ref, o_ref, sa_ref, sb_ref, m_ref, acc_ref, *, tq, v_head):
    qi = pl.program_id(2)
    n_blocks = pl.num_programs(2)
    n_hh = q_ref.shape[-1] // LANES
    key_minus_qry = (lax.broadcasted_iota(jnp.int32, (tq, tq), 0)
                     - lax.broadcasted_iota(jnp.int32, (tq, tq), 1))
    nt = (((1,), (1,)), ((), ()))
    qh = [q_ref[:, hh * LANES:(hh + 1) * LANES] for hh in range(n_hh)]

    def scores_into(dst_ref, ki):
        r = pl.multiple_of(ki * tq, tq)
        for hh in range(n_hh):
            kk = k_ref[pl.ds(r, tq), hh * LANES:(hh + 1) * LANES]
            dst_ref[hh] = lax.dot_general(kk, qh[hh], nt, preferred_element_type=F32)

    def accumulate(src_ref, ki, masked):
        for hh in range(n_hh):
            st = src_ref[hh]
            if masked:
                st = jnp.where(key_minus_qry <= (qi - ki) * tq, st, -jnp.inf)
            m = m_ref[hh]
            m_new = jnp.maximum(m, jnp.max(st, axis=0, keepdims=True))
            alpha = jnp.exp2(m - m_new)
            pt = jnp.exp2(st - m_new).astype(BF16)
            m_ref[hh] = m_new
            vt = vt_ref[ki, hh * VT_ROWS:(hh + 1) * VT_ROWS, :]
            acc_ref[hh] = alpha * acc_ref[hh] + jnp.dot(vt, pt, preferred_element_type=F32)

    m_ref[...] = jnp.full(m_ref.shape, -jnp.inf, F32)
    acc_ref[...] = jnp.zeros(acc_ref.shape, F32)

    scores_into(sa_ref, 0)

    def pair(kp, carry):
        scores_into(sb_ref, 2 * kp + 1)
        accumulate(sa_ref, 2 * kp, False)
        scores_into(sa_ref, 2 * kp + 2)
        accumulate(sb_ref, 2 * kp + 1, False)
        return carry

    lax.fori_loop(0, qi // 2, pair, 0)
    k0 = 2 * (qi // 2)
    scores_into(sb_ref, jnp.minimum(k0 + 1, n_blocks - 1))
    accumulate(sa_ref, k0, True)

    @pl.when(qi % 2 == 1)
    def _():
        accumulate(sb_ref, k0 + 1, True)

    ot = jnp.concatenate([acc_ref[hh, 0:v_head, :] / acc_ref[hh, v_head:v_head + 1, :] for hh in range(n_hh)],
                         axis=0)
    o_ref[...] = ot.T.astype(o_ref.dtype)


def _attn_prompt(q, k, vt, batch, seq, dm):
    tq = vt.shape[-1]
    nq = seq // tq
    hp = dm.n_heads // 2
    k3 = k.reshape(batch, seq, -1)
    vt4 = vt.reshape(batch, nq, vt.shape[1], tq)
    body = functools.partial(_attn_prompt_body, tq=tq, v_head=dm.v_head)
    return pl.pallas_call(
        body,
        grid=(batch, hp, nq),
        in_specs=[pl.BlockSpec((tq, 2 * LANES), lambda b, h, i: (b * nq + i, h)),
                  pl.BlockSpec((None, seq, 2 * LANES), lambda b, h, i: (b, 0, h)),
                  pl.BlockSpec((None, nq, 2 * VT_ROWS, tq), lambda b, h, i: (b, 0, h, 0))],
        out_specs=pl.BlockSpec((tq, 2 * dm.v_head), lambda b, h, i: (b * nq + i, h)),
        out_shape=jax.ShapeDtypeStruct((batch * seq, dm.n_heads * dm.v_head), BF16),
        scratch_shapes=[pltpu.VMEM((2, tq, tq), F32), pltpu.VMEM((2, tq, tq), F32),
                        pltpu.VMEM((2, 1, tq), F32), pltpu.VMEM((2, VT_ROWS, tq), F32)],
        compiler_params=_params("arbitrary", "arbitrary", "arbitrary"),
        name="attn_prompt",
    )(q, k3, vt4)


def _attn_out_body(x_ref, o_ref, wo_ref, gpost_ref, y_ref):
    h = jnp.dot(o_ref[...], wo_ref[...], preferred_element_type=F32)
    y_ref[...] = x_ref[...] + _rms(h, gpost_ref[...])


def _attn_out(x, o, wo, gpost, dm):
    n, d = x.shape
    tm = min(dm.tm, n)
    return pl.pallas_call(
        _attn_out_body,
        grid=(n // tm,),
        in_specs=[pl.BlockSpec((tm, d), lambda i: (i, 0)), pl.BlockSpec((tm, o.shape[1]), lambda i: (i, 0)),
                  _const_spec(wo.shape), _const_spec((1, d))],
        out_specs=pl.BlockSpec((tm, d), lambda i: (i, 0)),
        out_shape=jax.ShapeDtypeStruct((n, d), F32),
        compiler_params=_params("arbitrary"),
        name="attn_out",
    )(x, o, wo, gpost)


def _sample_q_body(q_ref, wukt_ref, ql_ref, qr_ref, *, n_heads, qk_nope, seq):
    n = q_ref.shape[0]
    lane = lax.broadcasted_iota(jnp.int32, (LANES, LANES), 1)
    src = lax.broadcasted_iota(jnp.int32, (LANES, LANES), 0)
    sel = (src == lane + qk_nope).astype(BF16)
    for hd in range(n_heads):
        qh = q_ref[:, hd * LANES:(hd + 1) * LANES]
        ql = jnp.dot(qh, wukt_ref[hd], preferred_element_type=F32)
        qr = jnp.dot(qh, sel, preferred_element_type=F32)
        ql_ref[:, hd] = ql.reshape(n // seq, seq, ql.shape[-1])
        qr_ref[:, hd] = qr.reshape(n // seq, seq, LANES)


def _sample_q(q, wuk_t, dec_batch, dec_seq, dm):
    c = dm.kv_lora
    body = functools.partial(_sample_q_body, n_heads=dm.n_heads, qk_nope=dm.qk_nope, seq=dec_seq)
    return pl.pallas_call(
        body,
        out_shape=[jax.ShapeDtypeStruct((dec_batch, dm.n_heads, dec_seq, c), F32),
                   jax.ShapeDtypeStruct((dec_batch, dm.n_heads, dec_seq, LANES), F32)],
        compiler_params=_params(),
        name="sample_q",
    )(q, wuk_t)


def _sample_attn_body(pt_ref, ql_ref, qr_ref, ckvn_ref, kpent_ref, ckv_hbm, kpet_hbm, o_ref,
                      m_ref, l_ref, acc_ref, cat_ref, kp_ref, kn_ref, ckv_buf, kpe_buf, sem, *, n_pages, seq):
    b = pl.program_id(0)
    j = pl.program_id(1)
    n_b = pl.num_programs(0)
    n_j = pl.num_programs(1)
    step = b * n_j + j
    slot = step % 2
    n_seq, rows, c = acc_ref.shape
    page = ckv_buf.shape[3]
    rope = kpe_buf.shape[3]

    def page_copies(bb, jj, sl, i, g):
        pg = pt_ref[bb * n_seq + i, jj * n_pages + g]
        return (pltpu.make_async_copy(ckv_hbm.at[pg], ckv_buf.at[sl, i, g], sem.at[sl, 0]),
                pltpu.make_async_copy(kpet_hbm.at[pg], kpe_buf.at[sl, i, g], sem.at[sl, 1]))

    def start_fetch(bb, jj, sl):
        for i in range(n_seq):
            for g in range(n_pages):
                for cp in page_copies(bb, jj, sl, i, g):
                    cp.start()

    def wait_fetch(bb, jj, sl):
        for i in range(n_seq):
            for g in range(n_pages):
                for cp in page_copies(bb, jj, sl, i, g):
                    cp.wait()

    @pl.when(step == 0)
    def _():
        start_fetch(0, 0, 0)

    last_j = j + 1 == n_j
    j_next = jnp.where(last_j, 0, j + 1)
    b_next = jnp.where(last_j, jnp.where(b + 1 == n_b, 0, b + 1), b)
    start_fetch(b_next, j_next, 1 - slot)

    ql =[ql_ref[i].reshape(rows, c).astype(BF16) for i in range(n_seq)]
    qr = [qr_ref[i].reshape(rows, LANES).astype(BF16) for i in range(n_seq)]
    nt = (((1,), (1,)), ((), ()))

    @pl.when(j == 0)
    def _():
        kp_ref[...] = jnp.zeros_like(kp_ref)
        kn_ref[...] = jnp.zeros_like(kn_ref)
        for i in range(n_seq):
            kn_ref[i, 0:rope, 0:seq] = kpent_ref[i]
            cn = jnp.concatenate([ckvn_ref[i], jnp.zeros((page - seq, c), F32)], axis=0).astype(BF16)
            s = (lax.dot_general(ql[i], cn, nt, preferred_element_type=F32)
                 + jnp.dot(qr[i], kn_ref[i].astype(BF16), preferred_element_type=F32))
            r = lax.broadcasted_iota(jnp.int32, s.shape, 0)
            k = lax.broadcasted_iota(jnp.int32, s.shape, 1)
            s = jnp.where(k <= r % seq, s, -jnp.inf)
            m = jnp.max(s, axis=-1, keepdims=True)
            p = jnp.exp2(s - m)
            m_ref[i] = m
            l_ref[i] = jnp.sum(p, axis=-1, keepdims=True)
            acc_ref[i] = jnp.dot(p.astype(BF16), cn, preferred_element_type=F32)

    wait_fetch(b, j, slot)
    n_chunks = 2 if n_pages % 2 == 0 else 1
    cp = n_pages // n_chunks * page
    for i in range(n_seq):
        s = []
        for ch in range(n_chunks):
            for g in range(ch * n_pages // n_chunks, (ch + 1) * n_pages // n_chunks):
                cat_ref[i, g * page:(g + 1) * page, :] = ckv_buf[slot, i, g].astype(BF16)
                kp_ref[i, 0:rope, g * page:(g + 1) * page] = kpe_buf[slot, i, g].astype(BF16)
            s.append(lax.dot_general(ql[i], cat_ref[i, ch * cp:(ch + 1) * cp, :], nt, preferred_element_type=F32)
                     + jnp.dot(qr[i], kp_ref[i, :, ch * cp:(ch + 1) * cp], preferred_element_type=F32))
        m = m_ref[i]
        m_new = m
        for sc in s:
            m_new = jnp.maximum(m_new, jnp.max(sc, axis=-1, keepdims=True))
        alpha = jnp.exp2(m - m_new)
        l = alpha * l_ref[i]
        acc = alpha * acc_ref[i]
        for ch, sc in enumerate(s):
            p = jnp.exp2(sc - m_new)
            l = l + jnp.sum(p, axis=-1, keepdims=True)
            acc = acc + jnp.dot(p.astype(BF16), cat_ref[i, ch * cp:(ch + 1) * cp, :], preferred_element_type=F32)
        m_ref[i] = m_new
        l_ref[i] = l
        acc_ref[i] = acc

    @pl.when(last_j)
    def _():
        o_ref[...] = acc_ref[...] / l_ref[...]

    @pl.when(step == n_b * n_j - 1)
    def _():
        wait_fetch(b_next, j_next, 1 - slot)


def _sample_attn(q_lat, q_rope, ckv_new, kpe_new_t, cache_ckv, cache_kpe_t, page_table, dm):
    bd, hn, seq, c = q_lat.shape
    n_log = page_table.shape[1]
    pps = min(dm.pages_per_step, n_log)
    ns = dm.seqs_per_step
    page = cache_ckv.shape[1]
    rope = cache_kpe_t.shape[1]
    rows = hn * seq
    body = functools.partial(_sample_attn_body, n_pages=pps, seq=seq)
    grid_spec = pltpu.PrefetchScalarGridSpec(
        num_scalar_prefetch=1,
        grid=(bd // ns, n_log // pps),
        in_specs=[pl.BlockSpec((ns, hn, seq, c), lambda b, j, pt: (b, 0, 0, 0)),
                  pl.BlockSpec((ns, hn, seq, LANES), lambda b, j, pt: (b, 0, 0, 0)),
                  pl.BlockSpec((ns, seq, c), lambda b, j, pt: (b, 0, 0)),
                  pl.BlockSpec((ns, rope, seq), lambda b, j, pt: (b, 0, 0)),
                  pl.BlockSpec(memory_space=pl.ANY), pl.BlockSpec(memory_space=pl.ANY)],
        out_specs=pl.BlockSpec((ns, rows, c), lambda b, j, pt: (b, 0, 0)),
        scratch_shapes=[pltpu.VMEM((ns, rows, 1), F32), pltpu.VMEM((ns, rows, 1), F32), pltpu.VMEM((ns, rows, c), F32),
                        pltpu.VMEM((ns, pps * page, c), BF16), pltpu.VMEM((ns, LANES, pps * page), BF16),
                        pltpu.VMEM((ns, LANES, page), F32),
                        pltpu.VMEM((2, ns, pps, page, c), F32), pltpu.VMEM((2, ns, pps, rope, page), F32),
                        pltpu.SemaphoreType.DMA((2, 2))],
    )
    return pl.pallas_call(
        body,
        grid_spec=grid_spec,
        out_shape=jax.ShapeDtypeStruct((bd, rows, c), F32),
        compiler_params=_params("arbitrary", "arbitrary"),
        name="attn_sample",
    )(page_table, q_lat, q_rope, ckv_new, kpe_new_t, cache_ckv, cache_kpe_t)


def _sample_out_body(x_ref, ol_ref, wuv_ref, wo_ref, gpost_ref, y_ref, *, n_heads):
    bb, _, seq, c = ol_ref.shape
    h = None
    for hd in range(n_heads):
        ol = ol_ref[:, hd].reshape(bb * seq, c).astype(BF16)
        oh = jnp.dot(ol, wuv_ref[hd], preferred_element_type=F32).astype(BF16)
        part = jnp.dot(oh, wo_ref[hd], preferred_element_type=F32)
        h = part if h is None else h + part
    y_ref[...] = x_ref[...] + _rms(h, gpost_ref[...])


def _sample_out(x, o_lat, wuv_p, wo_p, gpost, dm):
    n, d = x.shape
    bd, hn, seq, c = o_lat.shape
    bb = min(dm.out_bb, bd)
    body = functools.partial(_sample_out_body, n_heads=hn)
    return pl.pallas_call(
        body,
        grid=(bd // bb,),
        in_specs=[pl.BlockSpec((bb * seq, d), lambda i: (i, 0)),
                  pl.BlockSpec((bb, hn, seq, c), lambda i: (i, 0, 0, 0)),
                  _const_spec(wuv_p.shape), _const_spec(wo_p.shape), _const_spec((1, d))],
        out_specs=pl.BlockSpec((bb * seq, d), lambda i: (i, 0)),
        out_shape=jax.ShapeDtypeStruct((n, d), F32),
        compiler_params=_params("arbitrary"),
        name="sample_out",
    )(x, o_lat, wuv_p, wo_p, gpost)


def _forward(x, pos_tab, h0_re, h0_im, attend, w, dm, *, s5_tc, s5_bb):
    b, t, d = x.shape
    n = b * t
    xf = x.reshape(n, d)
    g = lambda a, i, k: a[i, k][None, :]
    xf = _ffn(xf, g(w['norm_pre'], 0, 0), g(w['norm_post'], 0, 0), *w['ffn'], 0, 0, dm)
    y3, hfin = _s5(xf.reshape(b, t, d), _state_to_slabs(h0_re, h0_im), g(w['norm_pre'], 0, 1),
                   g(w['norm_post'], 0, 1), w['bblk'], w['cw'], w['a_re'], w['a_im'], w['ssm_d'], w['wglu'], dm,
                   tc=s5_tc, bb=s5_bb)
    xf = y3.reshape(n, d)
    s_re, s_im = _slabs_to_state(hfin, d // dm.ssm_group, dm.ssm_state)
    xf = _ffn(xf, g(w['norm_pre'], 0, 2), g(w['norm_post'], 0, 2), *w['ffn'], 0, 1, dm)
    xf = _ffn(xf, g(w['norm_pre'], 1, 0), g(w['norm_post'], 1, 0), *w['ffn'], 1, 0, dm)
    xf, c_kv, k_pe = attend(xf, pos_tab, g(w['norm_pre'], 1, 1), g(w['norm_post'], 1, 1))
    xf = _ffn(xf, g(w['norm_pre'], 1, 2), g(w['norm_post'], 1, 2), *w['ffn'], 1, 1, dm)
    return (xf.reshape(b, t, d), s_re[None], s_im[None],
            c_kv.reshape(1, b, t, -1), k_pe.reshape(1, b, t, -1))


def _run(dm, x_prompt, x_sample, state_ssm_re, state_ssm_im, cache_kv_latent, cache_k_rope, page_table,
         norm_pre, norm_post, ffn_w_gate, ffn_w_up, ffn_w_down,
         ssm_a_re, ssm_a_im, ssm_log_dt, ssm_b_re, ssm_b_im, ssm_c_re, ssm_c_im, ssm_d, ssm_w_glu,
         mla_w_in, mla_q_norm, mla_kv_norm, mla_w_uq, mla_w_ukv, mla_w_o):
    b_p, s_p, d = x_prompt.shape
    b_s, s_s, _ = x_sample.shape
    past_len = page_table.shape[1] * dm.page_size
    depth = norm_pre.shape[0]
    assert depth == 2 and ssm_a_re.shape[0] == 1 and mla_w_in.shape[0] == 1

    ab_re, ab_im, bb_re, bb_im = _ssm_prep(ssm_a_re[0], ssm_a_im[0], ssm_log_dt[0], ssm_b_re[0], ssm_b_im[0])
    bblk, cw, a_re, a_im = _s5_weights(bb_re, bb_im, ssm_c_re[0], ssm_c_im[0], ab_re, ab_im, dm)
    mw = _mla_weights(mla_w_in[0], mla_w_uq[0], mla_w_ukv[0], mla_w_o[0], dm)
    w = dict(norm_pre=norm_pre, norm_post=norm_post,
             ffn=(ffn_w_gate.astype(BF16), ffn_w_up.astype(BF16), ffn_w_down.astype(BF16)),
             bblk=bblk, cw=cw, a_re=a_re, a_im=a_im, ssm_d=ssm_d[0][None, :], wglu=ssm_w_glu[0].astype(BF16))
    qn = mla_q_norm[0][None, :]
    kvn = mla_kv_norm[0][None, :]

    def prompt_attend(xf, tabs, gpre, gpost):
        q, c_kv, k_pe, kmat, vt = _mla_proj(xf, gpre, mw['win'], qn, kvn, mw['wuq'], mw['wuk'], mw['wuv'],
                                            mw['vone'], *tabs, dm, with_kv=True)
        o = _attn_prompt(q, kmat, vt, b_p, s_p, dm)
        return _attn_out(xf, o, mw['wo'], gpost, dm), c_kv, k_pe

    def sample_attend(xf, tabs, gpre, gpost):
        q, c_kv, k_pe = _mla_proj(xf, gpre, mw['win'], qn, kvn, mw['wuq'], mw['wuk'], mw['wuv'],
                                  mw['vone'], *tabs, dm, with_kv=False)
        q_lat, q_rope = _sample_q(q, mw['wuk_t'], b_s, s_s, dm)
        o_lat = _sample_attn(q_lat, q_rope, c_kv.reshape(b_s, s_s, -1), k_pe.reshape(b_s, s_s, -1).swapaxes(1, 2),
                             cache_kv_latent[0], cache_k_rope[0].swapaxes(1, 2), page_table, dm)
        o_lat = o_lat.reshape(b_s, dm.n_heads, s_s, -1)
        return _sample_out(xf, o_lat, mw['wuv_p'], mw['wo_p'], gpost, dm), c_kv, k_pe

    tm_p = min(dm.tm, b_p * s_p)
    tabs_p = _rope_tables(jnp.arange(max(s_p, tm_p), dtype=jnp.int32) % s_p, dm)
    h0 = jnp.zeros((b_p,) + state_ssm_re.shape[2:], F32)
    tc_p = max(dm.s5_rows // b_p, 1)
    out_p = _forward(x_prompt, tabs_p, h0, h0, prompt_attend, w, dm, s5_tc=min(tc_p, s_p), s5_bb=b_p)
    tm_s = min(dm.tm, b_s * s_s)
    tabs_s = _rope_tables(past_len + jnp.arange(max(s_s, tm_s), dtype=jnp.int32) % s_s, dm)
    bb_s = min(max(dm.s5_rows // s_s, SUBLANES), b_s)
    out_s = _forward(x_sample, tabs_s, state_ssm_re[0], state_ssm_im[0], sample_attend, w, dm,
                     s5_tc=s_s, s5_bb=bb_s)
    y_p, re_p, im_p, lat_p, kpe_p = out_p
    y_s, re_s, im_s, lat_s, kpe_s = out_s
    return (y_p, y_s, re_p, im_p, re_s, im_s, lat_p, kpe_p, lat_s, kpe_s)


def kernel(x_prompt, x_sample, state_ssm_re, state_ssm_im, cache_kv_latent, cache_k_rope, page_table,
           norm_pre, norm_post, ffn_w_gate, ffn_w_up, ffn_w_down,
           ssm_a_re, ssm_a_im, ssm_log_dt, ssm_b_re, ssm_b_im, ssm_c_re, ssm_c_im, ssm_d, ssm_w_glu,
           mla_w_in, mla_q_norm, mla_kv_norm, mla_w_uq, mla_w_ukv, mla_w_o):
    return _run(DIMS, x_prompt, x_sample, state_ssm_re, state_ssm_im, cache_kv_latent, cache_k_rope, page_table,
                norm_pre, norm_post, ffn_w_gate, ffn_w_up, ffn_w_down,
                ssm_a_re, ssm_a_im, ssm_log_dt, ssm_b_re, ssm_b_im, ssm_c_re, ssm_c_im, ssm_d, ssm_w_glu,
                mla_w_in, mla_q_norm, mla_kv_norm, mla_w_uq, mla_w_ukv, mla_w_o)
```

```python
import functools
import math
from typing import NamedTuple

import jax
import jax.numpy as jnp
from jax import lax
from jax.experimental import pallas as pl
from jax.experimental.pallas import tpu as pltpu

F32 = jnp.float32
BF16 = jnp.bfloat16

LANES = 128
VT_ROWS = 80
SUBLANES = 8
MXU_DIM = 256
VMEM_LIMIT_BYTES = 56 * 1024 * 1024

EPS = 1e-6
ROPE_THETA = 10000.0


class Dims(NamedTuple):
    d_model: int = 1024
    d_ff: int = 2816
    ssm_group: int = 16
    ssm_state: int = 64
    n_heads: int = 16
    qk_nope: int = 64
    qk_rope: int = 32
    v_head: int = 64
    kv_lora: int = 256
    q_lora: int = 768
    page_size: int = 128
    tm: int = 512
    ffn_chunk: int = 1536
    ffn_row_split: int = 2
    s5_rows: int = 256
    attn_heads: int = 4
    pages_per_step: int = 16
    seqs_per_step: int = 2
    out_bb: int = 32


DIMS = Dims()


def _rms(x, g):
    ms = jnp.mean(x * x, axis=-1, keepdims=True)
    return x * lax.rsqrt(ms + EPS) * g


def _const_spec(shape):
    zeros = (0,) * len(shape)
    return pl.BlockSpec(shape, lambda *_: zeros, pipeline_mode=pl.Buffered(1))


def _params(*sem):
    return pltpu.CompilerParams(dimension_semantics=sem, vmem_limit_bytes=VMEM_LIMIT_BYTES)


def _ffn_body(x_ref, gpre_ref, gpost_ref, wg_ref, wu_ref, wd_ref, o_ref, *, d_ff, chunk, row_split):
    rows = x_ref.shape[0] // row_split
    for r in range(row_split):
        rs = slice(r * rows, (r + 1) * rows)
        x = x_ref[rs, :]
        h = _rms(x, gpre_ref[...]).astype(BF16)
        y = None
        for c0 in range(0, d_ff, chunk):
            sl = slice(c0, min(c0 + chunk, d_ff))
            g = jnp.dot(h, wg_ref[:, sl], preferred_element_type=F32)
            u = jnp.dot(h, wu_ref[:, sl], preferred_element_type=F32)
            a = (g * jax.nn.sigmoid(g) * u).astype(BF16)
            part = jnp.dot(a, wd_ref[sl, :], preferred_element_type=F32)
            y = part if y is None else y + part
        o_ref[rs, :] = x + 0.5 * _rms(y, gpost_ref[...])


def _ffn(x, gpre, gpost, wg, wu, wd, layer, which, dm):
    n, d = x.shape
    tm = min(dm.tm, n)
    body = functools.partial(_ffn_body, d_ff=dm.d_ff, chunk=dm.ffn_chunk, row_split=dm.ffn_row_split)

    def weight(w):
        return pl.BlockSpec((None, None) + w.shape[2:], lambda i: (layer, which, 0, 0), pipeline_mode=pl.Buffered(1))

    return pl.pallas_call(
        body,
        grid=(n // tm,),
        in_specs=[pl.BlockSpec((tm, d), lambda i: (i, 0)),
                  _const_spec((1, d)), _const_spec((1, d)), weight(wg), weight(wu), weight(wd)],
        out_specs=pl.BlockSpec((tm, d), lambda i: (i, 0)),
        out_shape=jax.ShapeDtypeStruct((n, d), F32),
        compiler_params=_params("arbitrary"),
        name="ffn",
    )(x, gpre, gpost, wg, wu, wd)


def _ssm_prep_body(are_ref, aim_ref, ldt_ref, bre_ref, bim_ref, abre_ref, abim_ref, bbre_ref, bbim_ref):
    dt = jnp.exp(ldt_ref[...])
    lr = are_ref[...]
    li = aim_ref[...]
    mag = jnp.exp(lr * dt)
    ang = li * dt
    ab_re = mag * jnp.cos(ang)
    ab_im = mag * jnp.sin(ang)
    nr = ab_re - 1.0
    den = lr * lr + li * li
    g_re = (nr * lr + ab_im * li) / den
    g_im = (ab_im * lr - nr * li) / den
    abre_ref[...] = ab_re
    abim_ref[...] = ab_im
    b_re = bre_ref[...]
    b_im = bim_ref[...]
    bbre_ref[...] = g_re * b_re - g_im * b_im
    bbim_ref[...] = g_re * b_im + g_im * b_re


def _ssm_prep(a_re, a_im, log_dt, b_re, b_im):
    g, p = a_re.shape
    i = b_re.shape[-1]
    n = g * p
    col = lambda a: a.reshape(n, 1)
    ldt = jnp.broadcast_to(log_dt[:, None], (g, p))
    shp = [jax.ShapeDtypeStruct((n, 1), F32)] * 2 + [jax.ShapeDtypeStruct((n, i), F32)] * 2
    ab_re, ab_im, bb_re, bb_im = pl.pallas_call(_ssm_prep_body, out_shape=shp, name="ssm_prep")(
        col(a_re), col(a_im), col(ldt), b_re.reshape(n, i), b_im.reshape(n, i))
    return ab_re.reshape(g, p), ab_im.reshape(g, p), bb_re.reshape(g, p, i), bb_im.reshape(g, p, i)


def _s5_body(x_ref, h0_ref, perm_ref, permt_ref, gpre_ref, gpost_ref, bblk_ref, cw_ref, are_ref, aim_ref, d_ref,
             wglu_ref, o_ref, hfin_ref, s_ref, hst_ref, *, tc, bb, n_ktile, slabs_per_ktile, slabs_per_out):
    rows = tc * bb
    d = x_ref.shape[-1]

    @pl.when(pl.program_id(1) == 0)
    def _():
        hst_ref[...] = h0_ref[...]

    x = x_ref[...].reshape(rows, d)
    u_bt = _rms(x, gpre_ref[...])
    hi = u_bt.astype(BF16)
    rest = u_bt - hi.astype(F32)
    mid = rest.astype(BF16)
    lo = (rest - mid.astype(F32)).astype(BF16)
    perm = perm_ref[...]
    ub_f32 = jnp.dot(perm, hi, preferred_element_type=F32)
    ub = ub_f32.astype(BF16)
    u = ub_f32 + jnp.dot(perm, mid, preferred_element_type=F32) + jnp.dot(perm, lo, preferred_element_type=F32)
    pieces = []
    acc = None
    for j in range(n_ktile):
        bu = jnp.dot(ub[:, j * MXU_DIM:(j + 1) * MXU_DIM], bblk_ref[j], preferred_element_type=F32)
        for q in range(slabs_per_ktile):
            s_ref[j * slabs_per_ktile + q] = bu[:, q * 2 * LANES:(q + 1) * 2 * LANES]
        for q in range(slabs_per_ktile):
            k = j * slabs_per_ktile + q
            ar = jnp.broadcast_to(are_ref[k], (SUBLANES, LANES))
            ai = jnp.broadcast_to(aim_ref[k], (SUBLANES, LANES))
            for r0 in range(0, bb, SUBLANES):
                sr = hst_ref[k, r0:r0 + SUBLANES, 0:LANES]
                si = hst_ref[k, r0:r0 + SUBLANES, LANES:2 * LANES]
                for t in range(tc):
                    r = t * bb + r0
                    xr = s_ref[k, r:r + SUBLANES, 0:LANES]
                    xi = s_ref[k, r:r + SUBLANES, LANES:2 * LANES]
                    sr, si = ar * sr - ai * si + xr, ar * si + ai * sr + xi
                    s_ref[k, r:r + SUBLANES, 0:LANES] = sr
                    s_ref[k, r:r + SUBLANES, LANES:2 * LANES] = si
                hst_ref[k, r0:r0 + SUBLANES, 0:LANES] = sr
                hst_ref[k, r0:r0 + SUBLANES, LANES:2 * LANES] = si
            part = jnp.dot(s_ref[k].astype(BF16), cw_ref[k], preferred_element_type=F32)
            acc = part if acc is None else acc + part
            if (k + 1) % slabs_per_out == 0:
                pieces.append(acc)
                acc = None
    hfin_ref[...] = hst_ref[...]
    y = jnp.concatenate(pieces, axis=-1) + d_ref[...] * u
    hg_tb = jax.nn.gelu(y).astype(BF16)
    hg = jnp.dot(permt_ref[...], hg_tb, preferred_element_type=F32).astype(BF16)
    z = jnp.dot(hg, wglu_ref[...], preferred_element_type=F32)
    mix = z[:, :d] * jax.nn.sigmoid(z[:, d:])
    o_ref[...] = (x + _rms(mix, gpost_ref[...])).reshape(bb, tc, d)


def _s5(x, h0, gpre, gpost, bblk, cw, a_re, a_im, dvec, wglu, dm, *, tc, bb):
    b, t, d = x.shape
    n_slab = cw.shape[0]
    n_ktile = bblk.shape[0]
    rows = tc * bb
    out_lanes_per_slab = (LANES // dm.ssm_state) * dm.ssm_group
    body = functools.partial(_s5_body, tc=tc, bb=bb, n_ktile=n_ktile, slabs_per_ktile=n_slab // n_ktile,
                             slabs_per_out=LANES // out_lanes_per_slab)
    r_out = jnp.arange(rows)
    perm = jax.nn.one_hot((r_out % bb) * tc + r_out // bb, rows, dtype=BF16)
    return pl.pallas_call(
        body,
        grid=(b // bb, t // tc),
        in_specs=[pl.BlockSpec((bb, tc, d), lambda bi, ti: (bi, ti, 0)),
                  pl.BlockSpec((n_slab, bb, 2 * LANES), lambda bi, ti: (0, bi, 0)),
                  _const_spec((rows, rows)), _const_spec((rows, rows)),
                  _const_spec((1, d)), _const_spec((1, d)),
                  _const_spec(bblk.shape), _const_spec(cw.shape),
                  _const_spec(a_re.shape), _const_spec(a_im.shape),
                  _const_spec((1, d)), _const_spec(wglu.shape)],
        out_specs=[pl.BlockSpec((bb, tc, d), lambda bi, ti: (bi, ti, 0)),
                   pl.BlockSpec((n_slab, bb, 2 * LANES), lambda bi, ti: (0, bi, 0))],
        out_shape=[jax.ShapeDtypeStruct((b, t, d), F32),
                   jax.ShapeDtypeStruct((n_slab, b, 2 * LANES), F32)],
        scratch_shapes=[pltpu.VMEM((n_slab, rows, 2 * LANES), F32),
                        pltpu.VMEM((n_slab, bb, 2 * LANES), F32)],
        compiler_params=_params("arbitrary", "arbitrary"),
        name="s5_mixer",
    )(x, h0, perm, perm.T, gpre, gpost, bblk, cw, a_re, a_im, dvec, wglu)


def _s5_weights(bb_re, bb_im, c_re, c_im, ab_re, ab_im, dm):
    g, p, i = bb_re.shape
    gps = LANES // p
    n_slab = g // gps
    g_tile = MXU_DIM // i
    n_ktile = g // g_tile
    eye_t = jnp.eye(g_tile, dtype=F32)

    def b_tiles(bm):
        bt = bm.transpose(0, 2, 1).reshape(n_ktile, g_tile, i, p)
        return jnp.einsum('jgip,gh->jgihp', bt, eye_t).reshape(n_ktile, g_tile * i, g_tile // gps, LANES)

    bblk = jnp.stack([b_tiles(bb_re), b_tiles(bb_im)], axis=3)
    bblk = bblk.reshape(n_ktile, g_tile * i, g_tile * p * 2).astype(BF16)

    spo = LANES // (gps * i)
    eye_g = jnp.eye(gps, dtype=F32)
    sel = jax.nn.one_hot(jnp.arange(n_slab) % spo, spo, dtype=F32)

    def c_slabs(cm):
        ct = cm.reshape(n_slab, gps, i, p)
        base = jnp.einsum('khip,hg->khpgi', ct, eye_g).reshape(n_slab, gps * p, gps * i)
        return jnp.einsum('krc,kq->krqc', base, sel).reshape(n_slab, LANES, LANES)

    cw = jnp.concatenate([c_slabs(c_re), -c_slabs(c_im)], axis=1).astype(BF16)
    a_re = ab_re.reshape(n_slab, 1, LANES)
    a_im = ab_im.reshape(n_slab, 1, LANES)
    return bblk, cw, a_re, a_im


def _state_to_slabs(h_re, h_im):
    b = h_re.shape[0]
    re = h_re.reshape(b, -1, LANES)
    im = h_im.reshape(b, -1, LANES)
    return jnp.concatenate([re, im], axis=-1).transpose(1, 0, 2)


def _slabs_to_state(hs, g, p):
    n_slab, b, _ = hs.shape
    hs = hs.transpose(1, 0, 2)
    re = hs[:, :, :LANES].reshape(b, g, p)
    im = hs[:, :, LANES:].reshape(b, g, p)
    return re, im


def _q_scale(dm):
    return (dm.qk_nope + dm.qk_rope) ** -0.5 * math.log2(math.e)


def _rope_block(blk, t1, t2):
    return blk * t1 + pltpu.roll(blk, 3 * LANES // 4, 1) * t2


def _mla_proj_body(x_ref, gpre_ref, win_ref, qn_ref, kvn_ref, wuq_ref, wuk_ref, wuv_ref, vone_ref, t1_ref, t2_ref,
                   q_ref, ckv_ref, kpe_ref, *kv_refs, q_lora, kv_lora, n_heads, qk_rope, q_scale):
    x = x_ref[...]
    h = _rms(x, gpre_ref[...]).astype(BF16)
    hin = jnp.dot(h, win_ref[...], preferred_element_type=F32)
    cq = _rms(hin[:, :q_lora], qn_ref[...]).astype(BF16)
    ckv = _rms(hin[:, q_lora:q_lora + kv_lora], kvn_ref[...])
    t1 = t1_ref[...]
    t2 = t2_ref[...]
    kpe_blk = _rope_block(hin[:, q_lora + kv_lora:], t1, t2)
    ckv_ref[...] = ckv
    kpe_ref[...] = pltpu.roll(kpe_blk, LANES // 2, 1)[:, :qk_rope]
    q = jnp.dot(cq, wuq_ref[...], preferred_element_type=F32)
    for hd in range(n_heads):
        sl = slice(hd * LANES, (hd + 1) * LANES)
        q_ref[:, sl] = (_rope_block(q[:, sl], t1, t2) * q_scale).astype(BF16)
    if kv_refs:
        k_ref, v_ref = kv_refs
        cb = ckv.astype(BF16)
        kn = jnp.dot(cb, wuk_ref[...], preferred_element_type=F32)
        for hd in range(n_heads):
            sl = slice(hd * LANES, (hd + 1) * LANES)
            k_ref[:, sl] = (kn[:, sl] + kpe_blk).astype(BF16)
        vt = lax.dot_general(wuv_ref[...], cb, (((1,), (1,)), ((), ())), preferred_element_type=F32)
        v_ref[...] = (vt + vone_ref[...]).astype(BF16)


def _mla_proj(x, gpre, win, qn, kvn, wuq, wuk, wuv, vone, t1, t2, dm, *, with_kv):
    n, d = x.shape
    tm = min(dm.tm, n)
    n_tab = t1.shape[0] // tm
    hq = dm.n_heads * LANES
    hv = wuv.shape[0]
    body = functools.partial(_mla_proj_body, q_lora=dm.q_lora, kv_lora=dm.kv_lora, n_heads=dm.n_heads,
                             qk_rope=dm.qk_rope, q_scale=_q_scale(dm))
    row = lambda w: pl.BlockSpec((tm, w), lambda i: (i, 0))
    tab = pl.BlockSpec((tm, LANES), lambda i: (i % n_tab, 0))
    out_specs = [row(hq), row(dm.kv_lora), row(dm.qk_rope)]
    out_shape = [jax.ShapeDtypeStruct((n, hq), BF16), jax.ShapeDtypeStruct((n, dm.kv_lora), F32),
                 jax.ShapeDtypeStruct((n, dm.qk_rope), F32)]
    if with_kv:
        out_specs += [row(hq), pl.BlockSpec((None, hv, tm), lambda i: (i, 0, 0))]
        out_shape += [jax.ShapeDtypeStruct((n, hq), BF16), jax.ShapeDtypeStruct((n // tm, hv, tm), BF16)]
    return pl.pallas_call(
        body,
        grid=(n // tm,),
        in_specs=[row(d), _const_spec((1, d)), _const_spec(win.shape), _const_spec(qn.shape),
                  _const_spec(kvn.shape), _const_spec(wuq.shape), _const_spec(wuk.shape),
                  _const_spec(wuv.shape), _const_spec(vone.shape), tab, tab],
        out_specs=out_specs,
        out_shape=out_shape,
        compiler_params=_params("arbitrary"),
        name="mla_proj_kv" if with_kv else "mla_proj",
    )(x, gpre, win, qn, kvn, wuq, wuk, wuv, vone, t1, t2)


def _rope_tables(pos, dm):
    half = dm.qk_rope // 2
    inv = ROPE_THETA ** (-jnp.arange(0, dm.qk_rope, 2, dtype=F32) / dm.qk_rope)
    ang = pos.astype(F32)[:, None] * inv[None, :]
    cos, sin = jnp.cos(ang), jnp.sin(ang)
    n = pos.shape[0]
    pad = LANES - dm.qk_nope - dm.qk_rope
    t1 = jnp.concatenate([jnp.ones((n, dm.qk_nope), F32), cos, cos, jnp.zeros((n, pad), F32)], axis=1)
    t2 = jnp.concatenate([jnp.zeros((n, dm.qk_nope), F32), -sin, sin, jnp.zeros((n, pad), F32)], axis=1)
    del half
    return t1, t2


def _mla_weights(w_in, w_uq, w_ukv, w_o, dm):
    hn, nope, rope, vh = dm.n_heads, dm.qk_nope, dm.qk_rope, dm.v_head
    half = rope // 2
    d = w_in.shape[0]
    lat = dm.q_lora + dm.kv_lora
    kpe = w_in[:, lat:]
    kpe_sw = jnp.concatenate([kpe[:, half:], kpe[:, :half]], axis=1)
    win = jnp.concatenate([w_in[:, :lat], jnp.zeros((d, nope), F32), kpe, kpe_sw], axis=1).astype(BF16)
    uq = w_uq.reshape(dm.q_lora, hn, nope + rope)
    wuq = jnp.concatenate([uq, uq[:, :, nope + half:], uq[:, :, nope:nope + half]], axis=2)
    wuq = wuq.reshape(dm.q_lora, hn * LANES).astype(BF16)
    ukv = w_ukv.reshape(dm.kv_lora, hn, nope + vh)
    wuk = jnp.concatenate([ukv[:, :, :nope], jnp.zeros((dm.kv_lora, hn, LANES - nope), F32)], axis=2)
    wuk = wuk.reshape(dm.kv_lora, hn * LANES).astype(BF16)
    wuv = jnp.concatenate([ukv[:, :, nope:].transpose(1, 2, 0), jnp.zeros((hn, VT_ROWS - vh, dm.kv_lora), F32)],
                          axis=1).reshape(hn * VT_ROWS, dm.kv_lora).astype(BF16)
    vone = jnp.tile((jnp.arange(VT_ROWS) == vh).astype(F32), hn)[:, None]
    wuk_t = jnp.concatenate([ukv[:, :, :nope].transpose(1, 2, 0),
                             jnp.zeros((hn, LANES - nope, dm.kv_lora), F32)], axis=1).astype(BF16)
    wuv_p = jnp.concatenate([ukv[:, :, nope:].transpose(1, 0, 2),
                             jnp.zeros((hn, dm.kv_lora, LANES - vh), F32)], axis=2).astype(BF16)
    wo_p = jnp.concatenate([w_o.reshape(hn, vh, d), jnp.zeros((hn, LANES - vh, d), F32)], axis=1).astype(BF16)
    return dict(win=win, wuq=wuq, wuk=wuk, wuv=wuv, vone=vone, wuk_t=wuk_t, wuv_p=wuv_p, wo_p=wo_p, wo=w_o.astype(BF16))


def _attn_prompt_body(q_ref, k_ref, vt_ref, o_ref, sa_ref, sb_ref, m_ref, acc_ref, *, tq, v_head):
    qi = pl.program_id(2)
    n_blocks = pl.num_programs(2)
    n_hh = q_ref.shape[-1] // LANES
    key_minus_qry = (lax.broadcasted_iota(jnp.int32, (tq, tq), 0)
                     - lax.broadcasted_iota(jnp.int32, (tq, tq), 1))
    nt = (((1,), (1,)), ((), ()))
    qh = [q_ref[:, hh * LANES:(hh + 1) * LANES] for hh in range(n_hh)]

    def scores_into(dst_ref, ki):
        r = pl.multiple_of(ki * tq, tq)
        for hh in range(n_hh):
            kk = k_ref[pl.ds(r, tq), hh * LANES:(hh + 1) * LANES]
            dst_ref[hh] = lax.dot_general(kk, qh[hh], nt, preferred_element_type=F32)

    def accumulate(src_ref, ki, masked):
        for hh in range(n_hh):
            st = src_ref[hh]
            if masked:
                st = jnp.where(key_minus_qry <= (qi - ki) * tq, st, -jnp.inf)
            m = m_ref[hh]
            m_new = jnp.maximum(m, jnp.max(st, axis=0, keepdims=True))
            alpha = jnp.exp2(m - m_new)
            pt = jnp.exp2(st - m_new).astype(BF16)
            m_ref[hh] = m_new
            vt = vt_ref[ki, hh * VT_ROWS:(hh + 1) * VT_ROWS, :]
            acc_ref[hh] = alpha * acc_ref[hh] + jnp.dot(vt, pt, preferred_element_type=F32)

    m_ref[...] = jnp.full(m_ref.shape, -jnp.inf, F32)
    acc_ref[...] = jnp.zeros(acc_ref.shape, F32)

    scores_into(sa_ref, 0)

    def pair(kp, carry):
        scores_into(sb_ref, 2 * kp + 1)
        accumulate(sa_ref, 2 * kp, False)
        scores_into(sa_ref, 2 * kp + 2)
        accumulate(sb_ref, 2 * kp + 1, False)
        return carry

    lax.fori_loop(0, qi // 2, pair, 0)
    k0 = 2 * (qi // 2)
    scores_into(sb_ref, jnp.minimum(k0 + 1, n_blocks - 1))
    accumulate(sa_ref, k0, True)

    @pl.when(qi % 2 == 1)
    def _():
        accumulate(sb_ref, k0 + 1, True)

    ot = jnp.concatenate([acc_ref[hh, 0:v_head, :] / acc_ref[hh, v_head:v_head + 1, :] for hh in range(n_hh)],
                         axis=0)
    o_ref[...] = ot.T.astype(o_ref.dtype)


def _attn_prompt(q, k, vt, batch, seq, dm):
    tq = vt.shape[-1]
    nq = seq // tq
    nh = dm.attn_heads
    k3 = k.reshape(batch, seq, -1)
    vt4 = vt.reshape(batch, nq, vt.shape[1], tq)
    body = functools.partial(_attn_prompt_body, tq=tq, v_head=dm.v_head)
    return pl.pallas_call(
        body,
        grid=(batch, dm.n_heads // nh, nq),
        in_specs=[pl.BlockSpec((tq, nh * LANES), lambda b, h, i: (b * nq + i, h)),
                  pl.BlockSpec((None, seq, nh * LANES), lambda b, h, i: (b, 0, h)),
                  pl.BlockSpec((None, nq, nh * VT_ROWS, tq), lambda b, h, i: (b, 0, h, 0))],
        out_specs=pl.BlockSpec((tq, nh * dm.v_head), lambda b, h, i: (b * nq + i, h)),
        out_shape=jax.ShapeDtypeStruct((batch * seq, dm.n_heads * dm.v_head), BF16),
        scratch_shapes=[pltpu.VMEM((nh, tq, tq), F32), pltpu.VMEM((nh, tq, tq), F32),
                        pltpu.VMEM((nh, 1, tq), F32), pltpu.VMEM((nh, VT_ROWS, tq), F32)],
        compiler_params=_params("arbitrary", "arbitrary", "arbitrary"),
        name="attn_prompt",
    )(q, k3, vt4)


def _attn_out_body(x_ref, o_ref, wo_ref, gpost_ref, y_ref):
    h = jnp.dot(o_ref[...], wo_ref[...], preferred_element_type=F32)
    y_ref[...] = x_ref[...] + _rms(h, gpost_ref[...])


def _attn_out(x, o, wo, gpost, dm):
    n, d = x.shape
    tm = min(dm.tm, n)
    return pl.pallas_call(
        _attn_out_body,
        grid=(n // tm,),
        in_specs=[pl.BlockSpec((tm, d), lambda i: (i, 0)), pl.BlockSpec((tm, o.shape[1]), lambda i: (i, 0)),
                  _const_spec(wo.shape), _const_spec((1, d))],
        out_specs=pl.BlockSpec((tm, d), lambda i: (i, 0)),
        out_shape=jax.ShapeDtypeStruct((n, d), F32),
        compiler_params=_params("arbitrary"),
        name="attn_out",
    )(x, o, wo, gpost)


def _sample_q_body(q_ref, wukt_ref, ql_ref, qr_ref, *, n_heads, qk_nope, seq):
    n = q_ref.shape[0]
    lane = lax.broadcasted_iota(jnp.int32, (LANES, LANES), 1)
    src = lax.broadcasted_iota(jnp.int32, (LANES, LANES), 0)
    sel = (src == lane + qk_nope).astype(BF16)
    for hd in range(n_heads):
        qh = q_ref[:, hd * LANES:(hd + 1) * LANES]
        ql = jnp.dot(qh, wukt_ref[hd], preferred_element_type=F32)
        qr = jnp.dot(qh, sel, preferred_element_type=F32)
        ql_ref[:, hd] = ql.reshape(n // seq, seq, ql.shape[-1])
        qr_ref[:, hd] = qr.reshape(n // seq, seq, LANES)


def _sample_q(q, wuk_t, dec_batch, dec_seq, dm):
    c = dm.kv_lora
    body = functools.partial(_sample_q_body, n_heads=dm.n_heads, qk_nope=dm.qk_nope, seq=dec_seq)
    return pl.pallas_call(
        body,
        out_shape=[jax.ShapeDtypeStruct((dec_batch, dm.n_heads, dec_seq, c), F32),
                   jax.ShapeDtypeStruct((dec_batch, dm.n_heads, dec_seq, LANES), F32)],
        compiler_params=_params(),
        name="sample_q",
    )(q, wuk_t)


def _sample_attn_body(pt_ref, ql_ref, qr_ref, ckvn_ref, kpent_ref, ckv_hbm, kpet_hbm, o_ref,
                      m_ref, l_ref, acc_ref, cat_ref, kp_ref, kn_ref, ckv_buf, kpe_buf, sem, *, n_pages, seq):
    b = pl.program_id(0)
    j = pl.program_id(1)
    n_b = pl.num_programs(0)
    n_j = pl.num_programs(1)
    step = b * n_j + j
    slot = step % 2
    n_seq, rows, c = acc_ref.shape
    page = ckv_buf.shape[3]
    rope = kpe_buf.shape[3]

    def page_copies(bb, jj, sl, i, g):
        pg = pt_ref[bb * n_seq + i, jj * n_pages + g]
        return (pltpu.make_async_copy(ckv_hbm.at[pg], ckv_buf.at[sl, i, g], sem.at[sl, 0]),
                pltpu.make_async_copy(kpet_hbm.at[pg], kpe_buf.at[sl, i, g], sem.at[sl, 1]))

    def start_fetch(bb, jj, sl):
        for i in range(n_seq):
            for g in range(n_pages):
                for cp in page_copies(bb, jj, sl, i, g):
                    cp.start()

    def wait_fetch(bb, jj, sl):
        for i in range(n_seq):
            for g in range(n_pages):
                for cp in page_copies(bb, jj, sl, i, g):
                    cp.wait()

    @pl.when(step == 0)
    def _():
        start_fetch(0, 0, 0)

    last_j = j + 1 == n_j
    j_next = jnp.where(last_j, 0, j + 1)
    b_next = jnp.where(last_j, jnp.where(b + 1 == n_b, 0, b + 1), b)
    start_fetch(b_next, j_next, 1 - slot)

    ql =[ql_ref[i].reshape(rows, c).astype(BF16) for i in range(n_seq)]
    qr = [qr_ref[i].reshape(rows, LANES).astype(BF16) for i in range(n_seq)]
    nt = (((1,), (1,)), ((), ()))

    @pl.when(j == 0)
    def _():
        kp_ref[...] = jnp.zeros_like(kp_ref)
        kn_ref[...] = jnp.zeros_like(kn_ref)
        for i in range(n_seq):
            kn_ref[i, 0:rope, 0:seq] = kpent_ref[i]
            cn = jnp.concatenate([ckvn_ref[i], jnp.zeros((page - seq, c), F32)], axis=0).astype(BF16)
            s = (lax.dot_general(ql[i], cn, nt, preferred_element_type=F32)
                 + jnp.dot(qr[i], kn_ref[i].astype(BF16), preferred_element_type=F32))
            r = lax.broadcasted_iota(jnp.int32, s.shape, 0)
            k = lax.broadcasted_iota(jnp.int32, s.shape, 1)
            s = jnp.where(k <= r % seq, s, -jnp.inf)
            m = jnp.max(s, axis=-1, keepdims=True)
            p = jnp.exp2(s - m)
            m_ref[i] = m
            l_ref[i] = jnp.sum(p, axis=-1, keepdims=True)
            acc_ref[i] = jnp.dot(p.astype(BF16), cn, preferred_element_type=F32)

    wait_fetch(b, j, slot)
    n_chunks = 2 if n_pages % 2 == 0 else 1
    cp = n_pages // n_chunks * page
    for i in range(n_seq):
        s = []
        for ch in range(n_chunks):
            for g in range(ch * n_pages // n_chunks, (ch + 1) * n_pages // n_chunks):
                cat_ref[i, g * page:(g + 1) * page, :] = ckv_buf[slot, i, g].astype(BF16)
                kp_ref[i, 0:rope, g * page:(g + 1) * page] = kpe_buf[slot, i, g].astype(BF16)
            s.append(lax.dot_general(ql[i], cat_ref[i, ch * cp:(ch + 1) * cp, :], nt, preferred_element_type=F32)
                     + jnp.dot(qr[i], kp_ref[i, :, ch * cp:(ch + 1) * cp], preferred_element_type=F32))
        m = m_ref[i]
        m_new = m
        for sc in s:
            m_new = jnp.maximum(m_new, jnp.max(sc, axis=-1, keepdims=True))
        alpha = jnp.exp2(m - m_new)
        l = alpha * l_ref[i]
        acc = alpha * acc_ref[i]
        for ch, sc in enumerate(s):
            p = jnp.exp2(sc - m_new)
            l = l + jnp.sum(p, axis=-1, keepdims=True)
            acc = acc + jnp.dot(p.astype(BF16), cat_ref[i, ch * cp:(ch + 1) * cp, :], preferred_element_type=F32)
        m_ref[i] = m_new
        l_ref[i] = l
        acc_ref[i] = acc

    @pl.when(last_j)
    def _():
        o_ref[...] = acc_ref[...] / l_ref[...]

    @pl.when(step == n_b * n_j - 1)
    def _():
        wait_fetch(b_next, j_next, 1 - slot)


def _sample_attn(q_lat, q_rope, ckv_new, kpe_new_t, cache_ckv, cache_kpe_t, page_table, dm):
    bd, hn, seq, c = q_lat.shape
    n_log = page_table.shape[1]
    pps = min(dm.pages_per_step, n_log)
    ns = dm.seqs_per_step
    page = cache_ckv.shape[1]
    rope = cache_kpe_t.shape[1]
    rows = hn * seq
    body = functools.partial(_sample_attn_body, n_pages=pps, seq=seq)
    grid_spec = pltpu.PrefetchScalarGridSpec(
        num_scalar_prefetch=1,
        grid=(bd // ns, n_log // pps),
        in_specs=[pl.BlockSpec((ns, hn, seq, c), lambda b, j, pt: (b, 0, 0, 0)),
                  pl.BlockSpec((ns, hn, seq, LANES), lambda b, j, pt: (b, 0, 0, 0)),
                  pl.BlockSpec((ns, seq, c), lambda b, j, pt: (b, 0, 0)),
                  pl.BlockSpec((ns, rope, seq), lambda b, j, pt: (b, 0, 0)),
                  pl.BlockSpec(memory_space=pl.ANY), pl.BlockSpec(memory_space=pl.ANY)],
        out_specs=pl.BlockSpec((ns, rows, c), lambda b, j, pt: (b, 0, 0)),
        scratch_shapes=[pltpu.VMEM((ns, rows, 1), F32), pltpu.VMEM((ns, rows, 1), F32), pltpu.VMEM((ns, rows, c), F32),
                        pltpu.VMEM((ns, pps * page, c), BF16), pltpu.VMEM((ns, LANES, pps * page), BF16),
                        pltpu.VMEM((ns, LANES, page), F32),
                        pltpu.VMEM((2, ns, pps, page, c), F32), pltpu.VMEM((2, ns, pps, rope, page), F32),
                        pltpu.SemaphoreType.DMA((2, 2))],
    )
    return pl.pallas_call(
        body,
        grid_spec=grid_spec,
        out_shape=jax.ShapeDtypeStruct((bd, rows, c), F32),
        compiler_params=_params("arbitrary", "arbitrary"),
        name="attn_sample",
    )(page_table, q_lat, q_rope, ckv_new, kpe_new_t, cache_ckv, cache_kpe_t)


def _sample_out_body(x_ref, ol_ref, wuv_ref, wo_ref, gpost_ref, y_ref, *, n_heads):
    bb, _, seq, c = ol_ref.shape
    h = None
    for hd in range(n_heads):
        ol = ol_ref[:, hd].reshape(bb * seq, c).astype(BF16)
        oh = jnp.dot(ol, wuv_ref[hd], preferred_element_type=F32).astype(BF16)
        part = jnp.dot(oh, wo_ref[hd], preferred_element_type=F32)
        h = part if h is None else h + part
    y_ref[...] = x_ref[...] + _rms(h, gpost_ref[...])


def _sample_out(x, o_lat, wuv_p, wo_p, gpost, dm):
    n, d = x.shape
    bd, hn, seq, c = o_lat.shape
    bb = min(dm.out_bb, bd)
    body = functools.partial(_sample_out_body, n_heads=hn)
    return pl.pallas_call(
        body,
        grid=(bd // bb,),
        in_specs=[pl.BlockSpec((bb * seq, d), lambda i: (i, 0)),
                  pl.BlockSpec((bb, hn, seq, c), lambda i: (i, 0, 0, 0)),
                  _const_spec(wuv_p.shape), _const_spec(wo_p.shape), _const_spec((1, d))],
        out_specs=pl.BlockSpec((bb * seq, d), lambda i: (i, 0)),
        out_shape=jax.ShapeDtypeStruct((n, d), F32),
        compiler_params=_params("arbitrary"),
        name="sample_out",
    )(x, o_lat, wuv_p, wo_p, gpost)


def _forward(x, pos_tab, h0_re, h0_im, attend, w, dm, *, s5_tc, s5_bb):
    b, t, d = x.shape
    n = b * t
    xf = x.reshape(n, d)
    g = lambda a, i, k: a[i, k][None, :]
    xf = _ffn(xf, g(w['norm_pre'], 0, 0), g(w['norm_post'], 0, 0), *w['ffn'], 0, 0, dm)
    y3, hfin = _s5(xf.reshape(b, t, d), _state_to_slabs(h0_re, h0_im), g(w['norm_pre'], 0, 1),
                   g(w['norm_post'], 0, 1), w['bblk'], w['cw'], w['a_re'], w['a_im'], w['ssm_d'], w['wglu'], dm,
                   tc=s5_tc, bb=s5_bb)
    xf = y3.reshape(n, d)
    s_re, s_im = _slabs_to_state(hfin, d // dm.ssm_group, dm.ssm_state)
    xf = _ffn(xf, g(w['norm_pre'], 0, 2), g(w['norm_post'], 0, 2), *w['ffn'], 0, 1, dm)
    xf = _ffn(xf, g(w['norm_pre'], 1, 0), g(w['norm_post'], 1, 0), *w['ffn'], 1, 0, dm)
    xf, c_kv, k_pe = attend(xf, pos_tab, g(w['norm_pre'], 1, 1), g(w['norm_post'], 1, 1))
    xf = _ffn(xf, g(w['norm_pre'], 1, 2), g(w['norm_post'], 1, 2), *w['ffn'], 1, 1, dm)
    return (xf.reshape(b, t, d), s_re[None], s_im[None],
            c_kv.reshape(1, b, t, -1), k_pe.reshape(1, b, t, -1))


def _run(dm, x_prompt, x_sample, state_ssm_re, state_ssm_im, cache_kv_latent, cache_k_rope, page_table,
         norm_pre, norm_post, ffn_w_gate, ffn_w_up, ffn_w_down,
         ssm_a_re, ssm_a_im, ssm_log_dt, ssm_b_re, ssm_b_im, ssm_c_re, ssm_c_im, ssm_d, ssm_w_glu,
         mla_w_in, mla_q_norm, mla_kv_norm, mla_w_uq, mla_w_ukv, mla_w_o):
    b_p, s_p, d = x_prompt.shape
    b_s, s_s, _ = x_sample.shape
    past_len = page_table.shape[1] * dm.page_size
    depth = norm_pre.shape[0]
    assert depth == 2 and ssm_a_re.shape[0] == 1 and mla_w_in.shape[0] == 1

    ab_re, ab_im, bb_re, bb_im = _ssm_prep(ssm_a_re[0], ssm_a_im[0], ssm_log_dt[0], ssm_b_re[0], ssm_b_im[0])
    bblk, cw, a_re, a_im = _s5_weights(bb_re, bb_im, ssm_c_re[0], ssm_c_im[0], ab_re, ab_im, dm)
    mw = _mla_weights(mla_w_in[0], mla_w_uq[0], mla_w_ukv[0], mla_w_o[0], dm)
    w = dict(norm_pre=norm_pre, norm_post=norm_post,
             ffn=(ffn_w_gate.astype(BF16), ffn_w_up.astype(BF16), ffn_w_down.astype(BF16)),
             bblk=bblk, cw=cw, a_re=a_re, a_im=a_im, ssm_d=ssm_d[0][None, :], wglu=ssm_w_glu[0].astype(BF16))
    qn = mla_q_norm[0][None, :]
    kvn = mla_kv_norm[0][None, :]

    def prompt_attend(xf, tabs, gpre, gpost):
        q, c_kv, k_pe, kmat, vt = _mla_proj(xf, gpre, mw['win'], qn, kvn, mw['wuq'], mw['wuk'], mw['wuv'],
                                            mw['vone'], *tabs, dm, with_kv=True)
        o = _attn_prompt(q, kmat, vt, b_p, s_p, dm)
        return _attn_out(xf, o, mw['wo'], gpost, dm), c_kv, k_pe

    def sample_attend(xf, tabs, gpre, gpost):
        q, c_kv, k_pe = _mla_proj(xf, gpre, mw['win'], qn, kvn, mw['wuq'], mw['wuk'], mw['wuv'],
                                  mw['vone'], *tabs, dm, with_kv=False)
        q_lat, q_rope = _sample_q(q, mw['wuk_t'], b_s, s_s, dm)
        o_lat = _sample_attn(q_lat, q_rope, c_kv.reshape(b_s, s_s, -1), k_pe.reshape(b_s, s_s, -1).swapaxes(1, 2),
                             cache_kv_latent[0], cache_k_rope[0].swapaxes(1, 2), page_table, dm)
        o_lat = o_lat.reshape(b_s, dm.n_heads, s_s, -1)
        return _sample_out(xf, o_lat, mw['wuv_p'], mw['wo_p'], gpost, dm), c_kv, k_pe

    tm_p = min(dm.tm, b_p * s_p)
    tabs_p = _rope_tables(jnp.arange(max(s_p, tm_p), dtype=jnp.int32) % s_p, dm)
    h0 = jnp.zeros((b_p,) + state_ssm_re.shape[2:], F32)
    tc_p = max(dm.s5_rows // b_p, 1)
    out_p = _forward(x_prompt, tabs_p, h0, h0, prompt_attend, w, dm, s5_tc=min(tc_p, s_p), s5_bb=b_p)
    tm_s = min(dm.tm, b_s * s_s)
    tabs_s = _rope_tables(past_len + jnp.arange(max(s_s, tm_s), dtype=jnp.int32) % s_s, dm)
    bb_s = min(max(dm.s5_rows // s_s, SUBLANES), b_s)
    out_s = _forward(x_sample, tabs_s, state_ssm_re[0], state_ssm_im[0], sample_attend, w, dm,
                     s5_tc=s_s, s5_bb=bb_s)
    y_p, re_p, im_p, lat_p, kpe_p = out_p
    y_s, re_s, im_s, lat_s, kpe_s = out_s
    return (y_p, y_s, re_p, im_p, re_s, im_s, lat_p, kpe_p, lat_s, kpe_s)


def kernel(x_prompt, x_sample, state_ssm_re, state_ssm_im, cache_kv_latent, cache_k_rope, page_table,
           norm_pre, norm_post, ffn_w_gate, ffn_w_up, ffn_w_down,
           ssm_a_re, ssm_a_im, ssm_log_dt, ssm_b_re, ssm_b_im, ssm_c_re, ssm_c_im, ssm_d, ssm_w_glu,
           mla_w_in, mla_q_norm, mla_kv_norm, mla_w_uq, mla_w_ukv, mla_w_o):
    return _run(DIMS, x_prompt, x_sample, state_ssm_re, state_ssm_im, cache_kv_latent, cache_k_rope, page_table,
                norm_pre, norm_post, ffn_w_gate, ffn_w_up, ffn_w_down,
                ssm_a_re, ssm_a_im, ssm_log_dt, ssm_b_re, ssm_b_im, ssm_c_re, ssm_c_im, ssm_d, ssm_w_glu,
                mla_w_in, mla_q_norm, mla_kv_norm, mla_w_uq, mla_w_ukv, mla_w_o)
```

```python
import functools
import math
from typing import NamedTuple

import jax
import jax.numpy as jnp
from jax import lax
from jax.experimental import pallas as pl
from jax.experimental.pallas import tpu as pltpu

F32 = jnp.float32
BF16 = jnp.bfloat16

LANES = 128
VT_ROWS = 80
SUBLANES = 8
MXU_DIM = 256
VMEM_LIMIT_BYTES = 56 * 1024 * 1024

EPS = 1e-6
ROPE_THETA = 10000.0


class Dims(NamedTuple):
    d_model: int = 1024
    d_ff: int = 2816
    ssm_group: int = 16
    ssm_state: int = 64
    n_heads: int = 16
    qk_nope: int = 64
    qk_rope: int = 32
    v_head: int = 64
    kv_lora: int = 256
    q_lora: int = 768
    page_size: int = 128
    tm: int = 512
    ffn_chunk: int = 1536
    ffn_tm: int = 1024
    ffn_row_split: int = 4
    s5_rows: int = 256
    attn_heads: int = 4
    pages_per_step: int = 32
    page_chunks: int = 4
    seqs_per_step: int = 2
    out_bb: int = 32


DIMS = Dims()


def _rms(x, g):
    ms = jnp.mean(x * x, axis=-1, keepdims=True)
    return x * lax.rsqrt(ms + EPS) * g


def _const_spec(shape):
    zeros = (0,) * len(shape)
    return pl.BlockSpec(shape, lambda *_: zeros, pipeline_mode=pl.Buffered(1))


def _params(*sem):
    return pltpu.CompilerParams(dimension_semantics=sem, vmem_limit_bytes=VMEM_LIMIT_BYTES)


def _ffn_body(x_ref, gpre_ref, gpost_ref, wg_ref, wu_ref, wd_ref, o_ref, *, d_ff, chunk, row_split):
    rows = x_ref.shape[0] // row_split
    for r in range(row_split):
        rs = slice(r * rows, (r + 1) * rows)
        x = x_ref[rs, :]
        h = _rms(x, gpre_ref[...]).astype(BF16)
        y = None
        for c0 in range(0, d_ff, chunk):
            sl = slice(c0, min(c0 + chunk, d_ff))
            g = jnp.dot(h, wg_ref[:, sl], preferred_element_type=F32)
            u = jnp.dot(h, wu_ref[:, sl], preferred_element_type=F32)
            a = (g * jax.nn.sigmoid(g) * u).astype(BF16)
            part = jnp.dot(a, wd_ref[sl, :], preferred_element_type=F32)
            y = part if y is None else y + part
        o_ref[rs, :] = x + 0.5 * _rms(y, gpost_ref[...])


def _ffn(x, gpre, gpost, wg, wu, wd, layer, which, dm):
    n, d = x.shape
    tm = min(dm.ffn_tm, n)
    body = functools.partial(_ffn_body, d_ff=dm.d_ff, chunk=dm.ffn_chunk, row_split=dm.ffn_row_split)

    def weight(w):
        return pl.BlockSpec((None, None) + w.shape[2:], lambda i: (layer, which, 0, 0), pipeline_mode=pl.Buffered(1))

    return pl.pallas_call(
        body,
        grid=(n // tm,),
        in_specs=[pl.BlockSpec((tm, d), lambda i: (i, 0)),
                  _const_spec((1, d)), _const_spec((1, d)), weight(wg), weight(wu), weight(wd)],
        out_specs=pl.BlockSpec((tm, d), lambda i: (i, 0)),
        out_shape=jax.ShapeDtypeStruct((n, d), F32),
        compiler_params=_params("arbitrary"),
        name="ffn",
    )(x, gpre, gpost, wg, wu, wd)


def _ssm_prep_body(are_ref, aim_ref, ldt_ref, bre_ref, bim_ref, abre_ref, abim_ref, bbre_ref, bbim_ref):
    dt = jnp.exp(ldt_ref[...])
    lr = are_ref[...]
    li = aim_ref[...]
    mag = jnp.exp(lr * dt)
    ang = li * dt
    ab_re = mag * jnp.cos(ang)
    ab_im = mag * jnp.sin(ang)
    nr = ab_re - 1.0
    den = lr * lr + li * li
    g_re = (nr * lr + ab_im * li) / den
    g_im = (ab_im * lr - nr * li) / den
    abre_ref[...] = ab_re
    abim_ref[...] = ab_im
    b_re = bre_ref[...]
    b_im = bim_ref[...]
    bbre_ref[...] = g_re * b_re - g_im * b_im
    bbim_ref[...] = g_re * b_im + g_im * b_re


def _ssm_prep(a_re, a_im, log_dt, b_re, b_im):
    g, p = a_re.shape
    i = b_re.shape[-1]
    n = g * p
    col = lambda a: a.reshape(n, 1)
    ldt = jnp.broadcast_to(log_dt[:, None], (g, p))
    shp = [jax.ShapeDtypeStruct((n, 1), F32)] * 2 + [jax.ShapeDtypeStruct((n, i), F32)] * 2
    ab_re, ab_im, bb_re, bb_im = pl.pallas_call(_ssm_prep_body, out_shape=shp, name="ssm_prep")(
        col(a_re), col(a_im), col(ldt), b_re.reshape(n, i), b_im.reshape(n, i))
    return ab_re.reshape(g, p), ab_im.reshape(g, p), bb_re.reshape(g, p, i), bb_im.reshape(g, p, i)


def _s5_body(x_ref, h0_ref, perm_ref, permt_ref, gpre_ref, gpost_ref, bblk_ref, cw_ref, are_ref, aim_ref, d_ref,
             wglu_ref, o_ref, hfin_ref, s_ref, hst_ref, *, tc, bb, n_ktile, slabs_per_ktile, slabs_per_out):
    rows = tc * bb
    d = x_ref.shape[-1]

    @pl.when(pl.program_id(1) == 0)
    def _():
        hst_ref[...] = h0_ref[...]

    x = x_ref[...].reshape(rows, d)
    u_bt = _rms(x, gpre_ref[...])
    hi = u_bt.astype(BF16)
    rest = u_bt - hi.astype(F32)
    mid = rest.astype(BF16)
    lo = (rest - mid.astype(F32)).astype(BF16)
    perm = perm_ref[...]
    ub_f32 = jnp.dot(perm, hi, preferred_element_type=F32)
    ub = ub_f32.astype(BF16)
    u = ub_f32 + jnp.dot(perm, mid, preferred_element_type=F32) + jnp.dot(perm, lo, preferred_element_type=F32)
    pieces = []
    acc = None
    for j in range(n_ktile):
        bu = jnp.dot(ub[:, j * MXU_DIM:(j + 1) * MXU_DIM], bblk_ref[j], preferred_element_type=F32)
        for q in range(slabs_per_ktile):
            s_ref[j * slabs_per_ktile + q] = bu[:, q * 2 * LANES:(q + 1) * 2 * LANES]
        for q in range(slabs_per_ktile):
            k = j * slabs_per_ktile + q
            ar = jnp.broadcast_to(are_ref[k], (SUBLANES, LANES))
            ai = jnp.broadcast_to(aim_ref[k], (SUBLANES, LANES))
            for r0 in range(0, bb, SUBLANES):
                sr = hst_ref[k, r0:r0 + SUBLANES, 0:LANES]
                si = hst_ref[k, r0:r0 + SUBLANES, LANES:2 * LANES]
                for t in range(tc):
                    r = t * bb + r0
                    xr = s_ref[k, r:r + SUBLANES, 0:LANES]
                    xi = s_ref[k, r:r + SUBLANES, LANES:2 * LANES]
                    sr, si = ar * sr - ai * si + xr, ar * si + ai * sr + xi
                    s_ref[k, r:r + SUBLANES, 0:LANES] = sr
                    s_ref[k, r:r + SUBLANES, LANES:2 * LANES] = si
                hst_ref[k, r0:r0 + SUBLANES, 0:LANES] = sr
                hst_ref[k, r0:r0 + SUBLANES, LANES:2 * LANES] = si
            part = jnp.dot(s_ref[k].astype(BF16), cw_ref[k], preferred_element_type=F32)
            acc = part if acc is None else acc + part
            if (k + 1) % slabs_per_out == 0:
                pieces.append(acc)
                acc = None
    hfin_ref[...] = hst_ref[...]
    y = jnp.concatenate(pieces, axis=-1) + d_ref[...] * u
    hg_tb = jax.nn.gelu(y).astype(BF16)
    hg = jnp.dot(permt_ref[...], hg_tb, preferred_element_type=F32).astype(BF16)
    z = jnp.dot(hg, wglu_ref[...], preferred_element_type=F32)
    mix = z[:, :d] * jax.nn.sigmoid(z[:, d:])
    o_ref[...] = (x + _rms(mix, gpost_ref[...])).reshape(bb, tc, d)


def _s5(x, h0, gpre, gpost, bblk, cw, a_re, a_im, dvec, wglu, dm, *, tc, bb):
    b, t, d = x.shape
    n_slab = cw.shape[0]
    n_ktile = bblk.shape[0]
    rows = tc * bb
    out_lanes_per_slab = (LANES // dm.ssm_state) * dm.ssm_group
    body = functools.partial(_s5_body, tc=tc, bb=bb, n_ktile=n_ktile, slabs_per_ktile=n_slab // n_ktile,
                             slabs_per_out=LANES // out_lanes_per_slab)
    r_out = jnp.arange(rows)
    perm = jax.nn.one_hot((r_out % bb) * tc + r_out // bb, rows, dtype=BF16)
    return pl.pallas_call(
        body,
        grid=(b // bb, t // tc),
        in_specs=[pl.BlockSpec((bb, tc, d), lambda bi, ti: (bi, ti, 0)),
                  pl.BlockSpec((n_slab, bb, 2 * LANES), lambda bi, ti: (0, bi, 0)),
                  _const_spec((rows, rows)), _const_spec((rows, rows)),
                  _const_spec((1, d)), _const_spec((1, d)),
                  _const_spec(bblk.shape), _const_spec(cw.shape),
                  _const_spec(a_re.shape), _const_spec(a_im.shape),
                  _const_spec((1, d)), _const_spec(wglu.shape)],
        out_specs=[pl.BlockSpec((bb, tc, d), lambda bi, ti: (bi, ti, 0)),
                   pl.BlockSpec((n_slab, bb, 2 * LANES), lambda bi, ti: (0, bi, 0))],
        out_shape=[jax.ShapeDtypeStruct((b, t, d), F32),
                   jax.ShapeDtypeStruct((n_slab, b, 2 * LANES), F32)],
        scratch_shapes=[pltpu.VMEM((n_slab, rows, 2 * LANES), F32),
                        pltpu.VMEM((n_slab, bb, 2 * LANES), F32)],
        compiler_params=_params("arbitrary", "arbitrary"),
        name="s5_mixer",
    )(x, h0, perm, perm.T, gpre, gpost, bblk, cw, a_re, a_im, dvec, wglu)


def _s5_weights(bb_re, bb_im, c_re, c_im, ab_re, ab_im, dm):
    g, p, i = bb_re.shape
    gps = LANES // p
    n_slab = g // gps
    g_tile = MXU_DIM // i
    n_ktile = g // g_tile
    eye_t = jnp.eye(g_tile, dtype=F32)

    def b_tiles(bm):
        bt = bm.transpose(0, 2, 1).reshape(n_ktile, g_tile, i, p)
        return jnp.einsum('jgip,gh->jgihp', bt, eye_t).reshape(n_ktile, g_tile * i, g_tile // gps, LANES)

    bblk = jnp.stack([b_tiles(bb_re), b_tiles(bb_im)], axis=3)
    bblk = bblk.reshape(n_ktile, g_tile * i, g_tile * p * 2).astype(BF16)

    spo = LANES // (gps * i)
    eye_g = jnp.eye(gps, dtype=F32)
    sel = jax.nn.one_hot(jnp.arange(n_slab) % spo, spo, dtype=F32)

    def c_slabs(cm):
        ct = cm.reshape(n_slab, gps, i, p)
        base = jnp.einsum('khip,hg->khpgi', ct, eye_g).reshape(n_slab, gps * p, gps * i)
        return jnp.einsum('krc,kq->krqc', base, sel).reshape(n_slab, LANES, LANES)

    cw = jnp.concatenate([c_slabs(c_re), -c_slabs(c_im)], axis=1).astype(BF16)
    a_re = ab_re.reshape(n_slab, 1, LANES)
    a_im = ab_im.reshape(n_slab, 1, LANES)
    return bblk, cw, a_re, a_im


def _state_to_slabs(h_re, h_im):
    b = h_re.shape[0]
    re = h_re.reshape(b, -1, LANES)
    im = h_im.reshape(b, -1, LANES)
    return jnp.concatenate([re, im], axis=-1).transpose(1, 0, 2)


def _slabs_to_state(hs, g, p):
    n_slab, b, _ = hs.shape
    hs = hs.transpose(1, 0, 2)
    re = hs[:, :, :LANES].reshape(b, g, p)
    im = hs[:, :, LANES:].reshape(b, g, p)
    return re, im


def _q_scale(dm):
    return (dm.qk_nope + dm.qk_rope) ** -0.5 * math.log2(math.e)


def _rope_block(blk, t1, t2):
    return blk * t1 + pltpu.roll(blk, 3 * LANES // 4, 1) * t2


def _mla_proj_body(x_ref, gpre_ref, win_ref, qn_ref, kvn_ref, wuq_ref, wuk_ref, wuv_ref, vone_ref, t1_ref, t2_ref,
                   q_ref, ckv_ref, kpe_ref, *kv_refs, q_lora, kv_lora, n_heads, qk_rope, q_scale):
    x = x_ref[...]
    h = _rms(x, gpre_ref[...]).astype(BF16)
    hin = jnp.dot(h, win_ref[...], preferred_element_type=F32)
    cq = _rms(hin[:, :q_lora], qn_ref[...]).astype(BF16)
    ckv = _rms(hin[:, q_lora:q_lora + kv_lora], kvn_ref[...])
    t1 = t1_ref[...]
    t2 = t2_ref[...]
    kpe_blk = _rope_block(hin[:, q_lora + kv_lora:], t1, t2)
    ckv_ref[...] = ckv
    kpe_ref[...] = pltpu.roll(kpe_blk, LANES // 2, 1)[:, :qk_rope]
    q = jnp.dot(cq, wuq_ref[...], preferred_element_type=F32)
    for hd in range(n_heads):
        sl = slice(hd * LANES, (hd + 1) * LANES)
        q_ref[:, sl] = (_rope_block(q[:, sl], t1, t2) * q_scale).astype(BF16)
    if kv_refs:
        k_ref, v_ref = kv_refs
        cb = ckv.astype(BF16)
        kn = jnp.dot(cb, wuk_ref[...], preferred_element_type=F32)
        for hd in range(n_heads):
            sl = slice(hd * LANES, (hd + 1) * LANES)
            k_ref[:, sl] = (kn[:, sl] + kpe_blk).astype(BF16)
        vt = lax.dot_general(wuv_ref[...], cb, (((1,), (1,)), ((), ())), preferred_element_type=F32)
        v_ref[...] = (vt + vone_ref[...]).astype(BF16)


def _mla_proj(x, gpre, win, qn, kvn, wuq, wuk, wuv, vone, t1, t2, dm, *, with_kv):
    n, d = x.shape
    tm = min(dm.tm, n)
    n_tab = t1.shape[0] // tm
    hq = dm.n_heads * LANES
    hv = wuv.shape[0]
    body = functools.partial(_mla_proj_body, q_lora=dm.q_lora, kv_lora=dm.kv_lora, n_heads=dm.n_heads,
                             qk_rope=dm.qk_rope, q_scale=_q_scale(dm))
    row = lambda w: pl.BlockSpec((tm, w), lambda i: (i, 0))
    tab = pl.BlockSpec((tm, LANES), lambda i: (i % n_tab, 0))
    out_specs = [row(hq), row(dm.kv_lora), row(dm.qk_rope)]
    out_shape = [jax.ShapeDtypeStruct((n, hq), BF16), jax.ShapeDtypeStruct((n, dm.kv_lora), F32),
                 jax.ShapeDtypeStruct((n, dm.qk_rope), F32)]
    if with_kv:
        out_specs += [row(hq), pl.BlockSpec((None, hv, tm), lambda i: (i, 0, 0))]
        out_shape += [jax.ShapeDtypeStruct((n, hq), BF16), jax.ShapeDtypeStruct((n // tm, hv, tm), BF16)]
    return pl.pallas_call(
        body,
        grid=(n // tm,),
        in_specs=[row(d), _const_spec((1, d)), _const_spec(win.shape), _const_spec(qn.shape),
                  _const_spec(kvn.shape), _const_spec(wuq.shape), _const_spec(wuk.shape),
                  _const_spec(wuv.shape), _const_spec(vone.shape), tab, tab],
        out_specs=out_specs,
        out_shape=out_shape,
        compiler_params=_params("arbitrary"),
        name="mla_proj_kv" if with_kv else "mla_proj",
    )(x, gpre, win, qn, kvn, wuq, wuk, wuv, vone, t1, t2)


def _rope_tables(pos, dm):
    half = dm.qk_rope // 2
    inv = ROPE_THETA ** (-jnp.arange(0, dm.qk_rope, 2, dtype=F32) / dm.qk_rope)
    ang = pos.astype(F32)[:, None] * inv[None, :]
    cos, sin = jnp.cos(ang), jnp.sin(ang)
    n = pos.shape[0]
    pad = LANES - dm.qk_nope - dm.qk_rope
    t1 = jnp.concatenate([jnp.ones((n, dm.qk_nope), F32), cos, cos, jnp.zeros((n, pad), F32)], axis=1)
    t2 = jnp.concatenate([jnp.zeros((n, dm.qk_nope), F32), -sin, sin, jnp.zeros((n, pad), F32)], axis=1)
    del half
    return t1, t2


def _mla_weights(w_in, w_uq, w_ukv, w_o, dm):
    hn, nope, rope, vh = dm.n_heads, dm.qk_nope, dm.qk_rope, dm.v_head
    half = rope // 2
    d = w_in.shape[0]
    lat = dm.q_lora + dm.kv_lora
    kpe = w_in[:, lat:]
    kpe_sw = jnp.concatenate([kpe[:, half:], kpe[:, :half]], axis=1)
    win = jnp.concatenate([w_in[:, :lat], jnp.zeros((d, nope), F32), kpe, kpe_sw], axis=1).astype(BF16)
    uq = w_uq.reshape(dm.q_lora, hn, nope + rope)
    wuq = jnp.concatenate([uq, uq[:, :, nope + half:], uq[:, :, nope:nope + half]], axis=2)
    wuq = wuq.reshape(dm.q_lora, hn * LANES).astype(BF16)
    ukv = w_ukv.reshape(dm.kv_lora, hn, nope + vh)
    wuk = jnp.concatenate([ukv[:, :, :nope], jnp.zeros((dm.kv_lora, hn, LANES - nope), F32)], axis=2)
    wuk = wuk.reshape(dm.kv_lora, hn * LANES).astype(BF16)
    wuv = jnp.concatenate([ukv[:, :, nope:].transpose(1, 2, 0), jnp.zeros((hn, VT_ROWS - vh, dm.kv_lora), F32)],
                          axis=1).reshape(hn * VT_ROWS, dm.kv_lora).astype(BF16)
    vone = jnp.tile((jnp.arange(VT_ROWS) == vh).astype(F32), hn)[:, None]
    wuk_t = jnp.concatenate([ukv[:, :, :nope].transpose(1, 2, 0),
                             jnp.zeros((hn, LANES - nope, dm.kv_lora), F32)], axis=1).astype(BF16)
    wuv_p = jnp.concatenate([ukv[:, :, nope:].transpose(1, 0, 2),
                             jnp.zeros((hn, dm.kv_lora, LANES - vh), F32)], axis=2).astype(BF16)
    wo_p = jnp.concatenate([w_o.reshape(hn, vh, d), jnp.zeros((hn, LANES - vh, d), F32)], axis=1).astype(BF16)
    return dict(win=win, wuq=wuq, wuk=wuk, wuv=wuv, vone=vone, wuk_t=wuk_t, wuv_p=wuv_p, wo_p=wo_p, wo=w_o.astype(BF16))


def _attn_prompt_body(q_ref, k_ref, vt_ref, o_ref, sa_ref, sb_ref, m_ref, acc_ref, *, tq, v_head):
    qi = pl.program_id(2)
    n_blocks = pl.num_programs(2)
    n_hh = q_ref.shape[-1] // LANES
    key_minus_qry = (lax.broadcasted_iota(jnp.int32, (tq, tq), 0)
                     - lax.broadcasted_iota(jnp.int32, (tq, tq), 1))
    nt = (((1,), (1,)), ((), ()))
    qh = [q_ref[:, hh * LANES:(hh + 1) * LANES] for hh in range(n_hh)]

    def scores_into(dst_ref, ki):
        r = pl.multiple_of(ki * tq, tq)
        for hh in range(n_hh):
            kk = k_ref[pl.ds(r, tq), hh * LANES:(hh + 1) * LANES]
            dst_ref[hh] = lax.dot_general(kk, qh[hh], nt, preferred_element_type=F32)

    def accumulate(src_ref, ki, masked):
        for hh in range(n_hh):
            st = src_ref[hh]
            if masked:
                st = jnp.where(key_minus_qry <= (qi - ki) * tq, st, -jnp.inf)
            m = m_ref[hh]
            m_new = jnp.maximum(m, jnp.max(st, axis=0, keepdims=True))
            alpha = jnp.exp2(m - m_new)
            pt = jnp.exp2(st - m_new).astype(BF16)
            m_ref[hh] = m_new
            vt = vt_ref[ki, hh * VT_ROWS:(hh + 1) * VT_ROWS, :]
            acc_ref[hh] = alpha * acc_ref[hh] + jnp.dot(vt, pt, preferred_element_type=F32)

    m_ref[...] = jnp.full(m_ref.shape, -jnp.inf, F32)
    acc_ref[...] = jnp.zeros(acc_ref.shape, F32)

    scores_into(sa_ref, 0)

    def pair(kp, carry):
        scores_into(sb_ref, 2 * kp + 1)
        accumulate(sa_ref, 2 * kp, False)
        scores_into(sa_ref, 2 * kp + 2)
        accumulate(sb_ref, 2 * kp + 1, False)
        return carry

    lax.fori_loop(0, qi // 2, pair, 0)
    k0 = 2 * (qi // 2)
    scores_into(sb_ref, jnp.minimum(k0 + 1, n_blocks - 1))
    accumulate(sa_ref, k0, True)

    @pl.when(qi % 2 == 1)
    def _():
        accumulate(sb_ref, k0 + 1, True)

    ot = jnp.concatenate([acc_ref[hh, 0:v_head, :] / acc_ref[hh, v_head:v_head + 1, :] for hh in range(n_hh)],
                         axis=0)
    o_ref[...] = ot.T.astype(o_ref.dtype)


def _attn_prompt(q, k, vt, batch, seq, dm):
    tq = vt.shape[-1]
    nq = seq // tq
    nh = dm.attn_heads
    k3 = k.reshape(batch, seq, -1)
    vt4 = vt.reshape(batch, nq, vt.shape[1], tq)
    body = functools.partial(_attn_prompt_body, tq=tq, v_head=dm.v_head)
    return pl.pallas_call(
        body,
        grid=(batch, dm.n_heads // nh, nq),
        in_specs=[pl.BlockSpec((tq, nh * LANES), lambda b, h, i: (b * nq + i, h)),
                  pl.BlockSpec((None, seq, nh * LANES), lambda b, h, i: (b, 0, h)),
                  pl.BlockSpec((None, nq, nh * VT_ROWS, tq), lambda b, h, i: (b, 0, h, 0))],
        out_specs=pl.BlockSpec((tq, nh * dm.v_head), lambda b, h, i: (b * nq + i, h)),
        out_shape=jax.ShapeDtypeStruct((batch * seq, dm.n_heads * dm.v_head), BF16),
        scratch_shapes=[pltpu.VMEM((nh, tq, tq), F32), pltpu.VMEM((nh, tq, tq), F32),
                        pltpu.VMEM((nh, 1, tq), F32), pltpu.VMEM((nh, VT_ROWS, tq), F32)],
        compiler_params=_params("arbitrary", "arbitrary", "arbitrary"),
        name="attn_prompt",
    )(q, k3, vt4)


def _attn_out_body(x_ref, o_ref, wo_ref, gpost_ref, y_ref):
    h = jnp.dot(o_ref[...], wo_ref[...], preferred_element_type=F32)
    y_ref[...] = x_ref[...] + _rms(h, gpost_ref[...])


def _attn_out(x, o, wo, gpost, dm):
    n, d = x.shape
    tm = min(dm.tm, n)
    return pl.pallas_call(
        _attn_out_body,
        grid=(n // tm,),
        in_specs=[pl.BlockSpec((tm, d), lambda i: (i, 0)), pl.BlockSpec((tm, o.shape[1]), lambda i: (i, 0)),
                  _const_spec(wo.shape), _const_spec((1, d))],
        out_specs=pl.BlockSpec((tm, d), lambda i: (i, 0)),
        out_shape=jax.ShapeDtypeStruct((n, d), F32),
        compiler_params=_params("arbitrary"),
        name="attn_out",
    )(x, o, wo, gpost)


def _sample_q_body(q_ref, wukt_ref, ql_ref, qr_ref, *, n_heads, qk_nope, seq):
    n = q_ref.shape[0]
    lane = lax.broadcasted_iota(jnp.int32, (LANES, LANES), 1)
    src = lax.broadcasted_iota(jnp.int32, (LANES, LANES), 0)
    sel = (src == lane + qk_nope).astype(BF16)
    for hd in range(n_heads):
        qh = q_ref[:, hd * LANES:(hd + 1) * LANES]
        ql = jnp.dot(qh, wukt_ref[hd], preferred_element_type=F32)
        qr = jnp.dot(qh, sel, preferred_element_type=F32)
        ql_ref[:, hd] = ql.reshape(n // seq, seq, ql.shape[-1])
        qr_ref[:, hd] = qr.reshape(n // seq, seq, LANES)


def _sample_q(q, wuk_t, dec_batch, dec_seq, dm):
    c = dm.kv_lora
    body = functools.partial(_sample_q_body, n_heads=dm.n_heads, qk_nope=dm.qk_nope, seq=dec_seq)
    return pl.pallas_call(
        body,
        out_shape=[jax.ShapeDtypeStruct((dec_batch, dm.n_heads, dec_seq, c), F32),
                   jax.ShapeDtypeStruct((dec_batch, dm.n_heads, dec_seq, LANES), F32)],
        compiler_params=_params(),
        name="sample_q",
    )(q, wuk_t)


def _sample_attn_body(pt_ref, ql_ref, qr_ref, ckvn_ref, kpent_ref, ckv_hbm, kpet_hbm, o_ref,
                      m_ref, l_ref, acc_ref, cat_ref, kp_ref, kn_ref, ckv_buf, kpe_buf, sem, *, n_pages, seq,
                      n_chunks):
    b = pl.program_id(0)
    j = pl.program_id(1)
    n_b = pl.num_programs(0)
    n_j = pl.num_programs(1)
    step = b * n_j + j
    slot = step % 2
    n_seq, rows, c = acc_ref.shape
    page = ckv_buf.shape[3]
    rope = kpe_buf.shape[3]

    def page_copies(bb, jj, sl, i, g):
        pg = pt_ref[bb * n_seq + i, jj * n_pages + g]
        return (pltpu.make_async_copy(ckv_hbm.at[pg], ckv_buf.at[sl, i, g], sem.at[sl, 0]),
                pltpu.make_async_copy(kpet_hbm.at[pg], kpe_buf.at[sl, i, g], sem.at[sl, 1]))

    def start_fetch(bb, jj, sl):
        for i in range(n_seq):
            for g in range(n_pages):
                for cp in page_copies(bb, jj, sl, i, g):
                    cp.start()

    def wait_fetch(bb, jj, sl):
        for i in range(n_seq):
            for g in range(n_pages):
                for cp in page_copies(bb, jj, sl, i, g):
                    cp.wait()

    @pl.when(step == 0)
    def _():
        start_fetch(0, 0, 0)

    last_j = j + 1 == n_j
    j_next = jnp.where(last_j, 0, j + 1)
    b_next = jnp.where(last_j, jnp.where(b + 1 == n_b, 0, b + 1), b)
    start_fetch(b_next, j_next, 1 - slot)

    ql =[ql_ref[i].reshape(rows, c).astype(BF16) for i in range(n_seq)]
    qr = [qr_ref[i].reshape(rows, LANES).astype(BF16) for i in range(n_seq)]
    nt = (((1,), (1,)), ((), ()))

    @pl.when(j == 0)
    def _():
        kp_ref[...] = jnp.zeros_like(kp_ref)
        kn_ref[...] = jnp.zeros_like(kn_ref)
        for i in range(n_seq):
            kn_ref[i, 0:rope, 0:seq] = kpent_ref[i]
            cn = jnp.concatenate([ckvn_ref[i], jnp.zeros((page - seq, c), F32)], axis=0).astype(BF16)
            s = (lax.dot_general(ql[i], cn, nt, preferred_element_type=F32)
                 + jnp.dot(qr[i], kn_ref[i].astype(BF16), preferred_element_type=F32))
            r = lax.broadcasted_iota(jnp.int32, s.shape, 0)
            k = lax.broadcasted_iota(jnp.int32, s.shape, 1)
            s = jnp.where(k <= r % seq, s, -jnp.inf)
            m = jnp.max(s, axis=-1, keepdims=True)
            p = jnp.exp2(s - m)
            m_ref[i] = m
            l_ref[i] = jnp.sum(p, axis=-1, keepdims=True)
            acc_ref[i] = jnp.dot(p.astype(BF16), cn, preferred_element_type=F32)

    wait_fetch(b, j, slot)
    cp = n_pages // n_chunks * page
    for i in range(n_seq):
        s = []
        for ch in range(n_chunks):
            for g in range(ch * n_pages // n_chunks, (ch + 1) * n_pages // n_chunks):
                cat_ref[i, g * page:(g + 1) * page, :] = ckv_buf[slot, i, g].astype(BF16)
                kp_ref[i, 0:rope, g * page:(g + 1) * page] = kpe_buf[slot, i, g].astype(BF16)
            s.append(lax.dot_general(ql[i], cat_ref[i, ch * cp:(ch + 1) * cp, :], nt, preferred_element_type=F32)
                     + jnp.dot(qr[i], kp_ref[i, :, ch * cp:(ch + 1) * cp], preferred_element_type=F32))
        m = m_ref[i]
        m_new = m
        for sc in s:
            m_new = jnp.maximum(m_new, jnp.max(sc, axis=-1, keepdims=True))
        alpha = jnp.exp2(m - m_new)
        l = alpha * l_ref[i]
        acc = alpha * acc_ref[i]
        for ch, sc in enumerate(s):
            p = jnp.exp2(sc - m_new)
            l = l + jnp.sum(p, axis=-1, keepdims=True)
            acc = acc + jnp.dot(p.astype(BF16), cat_ref[i, ch * cp:(ch + 1) * cp, :], preferred_element_type=F32)
        m_ref[i] = m_new
        l_ref[i] = l
        acc_ref[i] = acc

    @pl.when(last_j)
    def _():
        o_ref[...] = acc_ref[...] / l_ref[...]

    @pl.when(step == n_b * n_j - 1)
    def _():
        wait_fetch(b_next, j_next, 1 - slot)


def _sample_attn(q_lat, q_rope, ckv_new, kpe_new_t, cache_ckv, cache_kpe_t, page_table, dm):
    bd, hn, seq, c = q_lat.shape
    n_log = page_table.shape[1]
    pps = min(dm.pages_per_step, n_log)
    ns = dm.seqs_per_step
    page = cache_ckv.shape[1]
    rope = cache_kpe_t.shape[1]
    rows = hn * seq
    n_chunks = math.gcd(dm.page_chunks, pps)
    body = functools.partial(_sample_attn_body, n_pages=pps, seq=seq, n_chunks=n_chunks)
    grid_spec = pltpu.PrefetchScalarGridSpec(
        num_scalar_prefetch=1,
        grid=(bd // ns, n_log // pps),
        in_specs=[pl.BlockSpec((ns, hn, seq, c), lambda b, j, pt: (b, 0, 0, 0)),
                  pl.BlockSpec((ns, hn, seq, LANES), lambda b, j, pt: (b, 0, 0, 0)),
                  pl.BlockSpec((ns, seq, c), lambda b, j, pt: (b, 0, 0)),
                  pl.BlockSpec((ns, rope, seq), lambda b, j, pt: (b, 0, 0)),
                  pl.BlockSpec(memory_space=pl.ANY), pl.BlockSpec(memory_space=pl.ANY)],
        out_specs=pl.BlockSpec((ns, rows, c), lambda b, j, pt: (b, 0, 0)),
        scratch_shapes=[pltpu.VMEM((ns, rows, 1), F32), pltpu.VMEM((ns, rows, 1), F32), pltpu.VMEM((ns, rows, c), F32),
                        pltpu.VMEM((ns, pps * page, c), BF16), pltpu.VMEM((ns, LANES, pps * page), BF16),
                        pltpu.VMEM((ns, LANES, page), F32),
                        pltpu.VMEM((2, ns, pps, page, c), F32), pltpu.VMEM((2, ns, pps, rope, page), F32),
                        pltpu.SemaphoreType.DMA((2, 2))],
    )
    return pl.pallas_call(
        body,
        grid_spec=grid_spec,
        out_shape=jax.ShapeDtypeStruct((bd, rows, c), F32),
        compiler_params=_params("arbitrary", "arbitrary"),
        name="attn_sample",
    )(page_table, q_lat, q_rope, ckv_new, kpe_new_t, cache_ckv, cache_kpe_t)


def _sample_out_body(x_ref, ol_ref, wuv_ref, wo_ref, gpost_ref, y_ref, *, n_heads):
    bb, _, seq, c = ol_ref.shape
    h = None
    for hd in range(n_heads):
        ol = ol_ref[:, hd].reshape(bb * seq, c).astype(BF16)
        oh = jnp.dot(ol, wuv_ref[hd], preferred_element_type=F32).astype(BF16)
        part = jnp.dot(oh, wo_ref[hd], preferred_element_type=F32)
        h = part if h is None else h + part
    y_ref[...] = x_ref[...] + _rms(h, gpost_ref[...])


def _sample_out(x, o_lat, wuv_p, wo_p, gpost, dm):
    n, d = x.shape
    bd, hn, seq, c = o_lat.shape
    bb = min(dm.out_bb, bd)
    body = functools.partial(_sample_out_body, n_heads=hn)
    return pl.pallas_call(
        body,
        grid=(bd // bb,),
        in_specs=[pl.BlockSpec((bb * seq, d), lambda i: (i, 0)),
                  pl.BlockSpec((bb, hn, seq, c), lambda i: (i, 0, 0, 0)),
                  _const_spec(wuv_p.shape), _const_spec(wo_p.shape), _const_spec((1, d))],
        out_specs=pl.BlockSpec((bb * seq, d), lambda i: (i, 0)),
        out_shape=jax.ShapeDtypeStruct((n, d), F32),
        compiler_params=_params("arbitrary"),
        name="sample_out",
    )(x, o_lat, wuv_p, wo_p, gpost)


def _forward(x, pos_tab, h0_re, h0_im, attend, w, dm, *, s5_tc, s5_bb):
    b, t, d = x.shape
    n = b * t
    xf = x.reshape(n, d)
    g = lambda a, i, k: a[i, k][None, :]
    xf = _ffn(xf, g(w['norm_pre'], 0, 0), g(w['norm_post'], 0, 0), *w['ffn'], 0, 0, dm)
    y3, hfin = _s5(xf.reshape(b, t, d), _state_to_slabs(h0_re, h0_im), g(w['norm_pre'], 0, 1),
                   g(w['norm_post'], 0, 1), w['bblk'], w['cw'], w['a_re'], w['a_im'], w['ssm_d'], w['wglu'], dm,
                   tc=s5_tc, bb=s5_bb)
    xf = y3.reshape(n, d)
    s_re, s_im = _slabs_to_state(hfin, d // dm.ssm_group, dm.ssm_state)
    xf = _ffn(xf, g(w['norm_pre'], 0, 2), g(w['norm_post'], 0, 2), *w['ffn'], 0, 1, dm)
    xf = _ffn(xf, g(w['norm_pre'], 1, 0), g(w['norm_post'], 1, 0), *w['ffn'], 1, 0, dm)
    xf, c_kv, k_pe = attend(xf, pos_tab, g(w['norm_pre'], 1, 1), g(w['norm_post'], 1, 1))
    xf = _ffn(xf, g(w['norm_pre'], 1, 2), g(w['norm_post'], 1, 2), *w['ffn'], 1, 1, dm)
    return (xf.reshape(b, t, d), s_re[None], s_im[None],
            c_kv.reshape(1, b, t, -1), k_pe.reshape(1, b, t, -1))


def _run(dm, x_prompt, x_sample, state_ssm_re, state_ssm_im, cache_kv_latent, cache_k_rope, page_table,
         norm_pre, norm_post, ffn_w_gate, ffn_w_up, ffn_w_down,
         ssm_a_re, ssm_a_im, ssm_log_dt, ssm_b_re, ssm_b_im, ssm_c_re, ssm_c_im, ssm_d, ssm_w_glu,
         mla_w_in, mla_q_norm, mla_kv_norm, mla_w_uq, mla_w_ukv, mla_w_o):
    b_p, s_p, d = x_prompt.shape
    b_s, s_s, _ = x_sample.shape
    past_len = page_table.shape[1] * dm.page_size
    depth = norm_pre.shape[0]
    assert depth == 2 and ssm_a_re.shape[0] == 1 and mla_w_in.shape[0] == 1

    ab_re, ab_im, bb_re, bb_im = _ssm_prep(ssm_a_re[0], ssm_a_im[0], ssm_log_dt[0], ssm_b_re[0], ssm_b_im[0])
    bblk, cw, a_re, a_im = _s5_weights(bb_re, bb_im, ssm_c_re[0], ssm_c_im[0], ab_re, ab_im, dm)
    mw = _mla_weights(mla_w_in[0], mla_w_uq[0], mla_w_ukv[0], mla_w_o[0], dm)
    w = dict(norm_pre=norm_pre, norm_post=norm_post,
             ffn=(ffn_w_gate.astype(BF16), ffn_w_up.astype(BF16), ffn_w_down.astype(BF16)),
             bblk=bblk, cw=cw, a_re=a_re, a_im=a_im, ssm_d=ssm_d[0][None, :], wglu=ssm_w_glu[0].astype(BF16))
    qn = mla_q_norm[0][None, :]
    kvn = mla_kv_norm[0][None, :]

    def prompt_attend(xf, tabs, gpre, gpost):
        q, c_kv, k_pe, kmat, vt = _mla_proj(xf, gpre, mw['win'], qn, kvn, mw['wuq'], mw['wuk'], mw['wuv'],
                                            mw['vone'], *tabs, dm, with_kv=True)
        o = _attn_prompt(q, kmat, vt, b_p, s_p, dm)
        return _attn_out(xf, o, mw['wo'], gpost, dm), c_kv, k_pe

    def sample_attend(xf, tabs, gpre, gpost):
        q, c_kv, k_pe = _mla_proj(xf, gpre, mw['win'], qn, kvn, mw['wuq'], mw['wuk'], mw['wuv'],
                                  mw['vone'], *tabs, dm, with_kv=False)
        q_lat, q_rope = _sample_q(q, mw['wuk_t'], b_s, s_s, dm)
        o_lat = _sample_attn(q_lat, q_rope, c_kv.reshape(b_s, s_s, -1), k_pe.reshape(b_s, s_s, -1).swapaxes(1, 2),
                             cache_kv_latent[0], cache_k_rope[0].swapaxes(1, 2), page_table, dm)
        o_lat = o_lat.reshape(b_s, dm.n_heads, s_s, -1)
        return _sample_out(xf, o_lat, mw['wuv_p'], mw['wo_p'], gpost, dm), c_kv, k_pe

    tm_p = min(dm.tm, b_p * s_p)
    tabs_p = _rope_tables(jnp.arange(max(s_p, tm_p), dtype=jnp.int32) % s_p, dm)
    h0 = jnp.zeros((b_p,) + state_ssm_re.shape[2:], F32)
    tc_p = max(dm.s5_rows // b_p, 1)
    out_p = _forward(x_prompt, tabs_p, h0, h0, prompt_attend, w, dm, s5_tc=min(tc_p, s_p), s5_bb=b_p)
    tm_s = min(dm.tm, b_s * s_s)
    tabs_s = _rope_tables(past_len + jnp.arange(max(s_s, tm_s), dtype=jnp.int32) % s_s, dm)
    bb_s = min(max(dm.s5_rows // s_s, SUBLANES), b_s)
    out_s = _forward(x_sample, tabs_s, state_ssm_re[0], state_ssm_im[0], sample_attend, w, dm,
                     s5_tc=s_s, s5_bb=bb_s)
    y_p, re_p, im_p, lat_p, kpe_p = out_p
    y_s, re_s, im_s, lat_s, kpe_s = out_s
    return (y_p, y_s, re_p, im_p, re_s, im_s, lat_p, kpe_p, lat_s, kpe_s)


def kernel(x_prompt, x_sample, state_ssm_re, state_ssm_im, cache_kv_latent, cache_k_rope, page_table,
           norm_pre, norm_post, ffn_w_gate, ffn_w_up, ffn_w_down,
           ssm_a_re, ssm_a_im, ssm_log_dt, ssm_b_re, ssm_b_im, ssm_c_re, ssm_c_im, ssm_d, ssm_w_glu,
           mla_w_in, mla_q_norm, mla_kv_norm, mla_w_uq, mla_w_ukv, mla_w_o):
    return _run(DIMS, x_prompt, x_sample, state_ssm_re, state_ssm_im, cache_kv_latent, cache_k_rope, page_table,
                norm_pre, norm_post, ffn_w_gate, ffn_w_up, ffn_w_down,
                ssm_a_re, ssm_a_im, ssm_log_dt, ssm_b_re, ssm_b_im, ssm_c_re, ssm_c_im, ssm_d, ssm_w_glu,
                mla_w_in, mla_q_norm, mla_kv_norm, mla_w_uq, mla_w_ukv, mla_w_o)
```

```python
import functools
import math
from typing import NamedTuple

import jax
import jax.numpy as jnp
from jax import lax
from jax.experimental import pallas as pl
from jax.experimental.pallas import tpu as pltpu

F32 = jnp.float32
BF16 = jnp.bfloat16

LANES = 128
VT_ROWS = 80
SUBLANES = 8
MXU_DIM = 256
VMEM_LIMIT_BYTES = 56 * 1024 * 1024

EPS = 1e-6
ROPE_THETA = 10000.0


class Dims(NamedTuple):
    d_model: int = 1024
    d_ff: int = 2816
    ssm_group: int = 16
    ssm_state: int = 64
    n_heads: int = 16
    qk_nope: int = 64
    qk_rope: int = 32
    v_head: int = 64
    kv_lora: int = 256
    q_lora: int = 768
    page_size: int = 128
    tm: int = 512
    ffn_chunk: int = 1536
    ffn_tm: int = 1024
    ffn_row_split: int = 4
    s5_rows: int = 256
    attn_heads: int = 4
    pages_per_step: int = 64
    page_chunks: int = 4
    seqs_per_step: int = 2
    out_bb: int = 32


DIMS = Dims()


def _rms(x, g):
    ms = jnp.mean(x * x, axis=-1, keepdims=True)
    return x * lax.rsqrt(ms + EPS) * g


def _const_spec(shape):
    zeros = (0,) * len(shape)
    return pl.BlockSpec(shape, lambda *_: zeros, pipeline_mode=pl.Buffered(1))


def _params(*sem):
    return pltpu.CompilerParams(dimension_semantics=sem, vmem_limit_bytes=VMEM_LIMIT_BYTES)


def _ffn_body(x_ref, gpre_ref, gpost_ref, wg_ref, wu_ref, wd_ref, *rest, d_ff, chunk, row_split):
    o_ref = rest[-1]
    rows = x_ref.shape[0] // row_split
    for r in range(row_split):
        rs = slice(r * rows, (r + 1) * rows)
        x = x_ref[rs, :]
        if len(rest) > 1:
            attn_ref, wo_ref, gattn_ref = rest[:3]
            proj = jnp.dot(attn_ref[rs, :], wo_ref[...], preferred_element_type=F32)
            x = x + _rms(proj, gattn_ref[...])
        h = _rms(x, gpre_ref[...]).astype(BF16)
        y = None
        for c0 in range(0, d_ff, chunk):
            sl = slice(c0, min(c0 + chunk, d_ff))
            g = jnp.dot(h, wg_ref[:, sl], preferred_element_type=F32)
            u = jnp.dot(h, wu_ref[:, sl], preferred_element_type=F32)
            a = (g * jax.nn.sigmoid(g) * u).astype(BF16)
            part = jnp.dot(a, wd_ref[sl, :], preferred_element_type=F32)
            y = part if y is None else y + part
        o_ref[rs, :] = x + 0.5 * _rms(y, gpost_ref[...])


def _ffn(x, gpre, gpost, wg, wu, wd, layer, which, dm, attn=None):
    n, d = x.shape
    tm = min(dm.ffn_tm, n)
    body = functools.partial(_ffn_body, d_ff=dm.d_ff, chunk=dm.ffn_chunk, row_split=dm.ffn_row_split)

    def weight(w):
        return pl.BlockSpec((None, None) + w.shape[2:], lambda i: (layer, which, 0, 0), pipeline_mode=pl.Buffered(1))

    in_specs = [pl.BlockSpec((tm, d), lambda i: (i, 0)),
                _const_spec((1, d)), _const_spec((1, d)), weight(wg), weight(wu), weight(wd)]
    operands = [x, gpre, gpost, wg, wu, wd]
    if attn is not None:
        o, wo, gattn = attn
        in_specs += [pl.BlockSpec((tm, o.shape[1]), lambda i: (i, 0)), _const_spec(wo.shape), _const_spec((1, d))]
        operands += [o, wo, gattn]
    return pl.pallas_call(
        body,
        grid=(n // tm,),
        in_specs=in_specs,
        out_specs=pl.BlockSpec((tm, d), lambda i: (i, 0)),
        out_shape=jax.ShapeDtypeStruct((n, d), F32),
        compiler_params=_params("arbitrary"),
        name="ffn_attn_out" if attn is not None else "ffn",
    )(*operands)


def _ssm_prep_body(are_ref, aim_ref, ldt_ref, bre_ref, bim_ref, abre_ref, abim_ref, bbre_ref, bbim_ref):
    dt = jnp.exp(ldt_ref[...])
    lr = are_ref[...]
    li = aim_ref[...]
    mag = jnp.exp(lr * dt)
    ang = li * dt
    ab_re = mag * jnp.cos(ang)
    ab_im = mag * jnp.sin(ang)
    nr = ab_re - 1.0
    den = lr * lr + li * li
    g_re = (nr * lr + ab_im * li) / den
    g_im = (ab_im * lr - nr * li) / den
    abre_ref[...] = ab_re
    abim_ref[...] = ab_im
    b_re = bre_ref[...]
    b_im = bim_ref[...]
    bbre_ref[...] = g_re * b_re - g_im * b_im
    bbim_ref[...] = g_re * b_im + g_im * b_re


def _ssm_prep(a_re, a_im, log_dt, b_re, b_im):
    g, p = a_re.shape
    i = b_re.shape[-1]
    n = g * p
    col = lambda a: a.reshape(n, 1)
    ldt = jnp.broadcast_to(log_dt[:, None], (g, p))
    shp = [jax.ShapeDtypeStruct((n, 1), F32)] * 2 + [jax.ShapeDtypeStruct((n, i), F32)] * 2
    ab_re, ab_im, bb_re, bb_im = pl.pallas_call(_ssm_prep_body, out_shape=shp, name="ssm_prep")(
        col(a_re), col(a_im), col(ldt), b_re.reshape(n, i), b_im.reshape(n, i))
    return ab_re.reshape(g, p), ab_im.reshape(g, p), bb_re.reshape(g, p, i), bb_im.reshape(g, p, i)


def _s5_body(x_ref, h0_ref, perm_ref, permt_ref, gpre_ref, gpost_ref, bblk_ref, cw_ref, are_ref, aim_ref, d_ref,
             wglu_ref, o_ref, hfin_ref, s_ref, hst_ref, *, tc, bb, n_ktile, slabs_per_ktile, slabs_per_out):
    rows = tc * bb
    d = x_ref.shape[-1]

    @pl.when(pl.program_id(1) == 0)
    def _():
        hst_ref[...] = h0_ref[...]

    x = x_ref[...].reshape(rows, d)
    u_bt = _rms(x, gpre_ref[...])
    hi = u_bt.astype(BF16)
    rest = u_bt - hi.astype(F32)
    mid = rest.astype(BF16)
    lo = (rest - mid.astype(F32)).astype(BF16)
    perm = perm_ref[...]
    ub_f32 = jnp.dot(perm, hi, preferred_element_type=F32)
    ub = ub_f32.astype(BF16)
    u = ub_f32 + jnp.dot(perm, mid, preferred_element_type=F32) + jnp.dot(perm, lo, preferred_element_type=F32)
    pieces = []
    acc = None
    for j in range(n_ktile):
        bu = jnp.dot(ub[:, j * MXU_DIM:(j + 1) * MXU_DIM], bblk_ref[j], preferred_element_type=F32)
        for q in range(slabs_per_ktile):
            s_ref[j * slabs_per_ktile + q] = bu[:, q * 2 * LANES:(q + 1) * 2 * LANES]
        for q in range(slabs_per_ktile):
            k = j * slabs_per_ktile + q
            ar = jnp.broadcast_to(are_ref[k], (SUBLANES, LANES))
            ai = jnp.broadcast_to(aim_ref[k], (SUBLANES, LANES))
            for r0 in range(0, bb, SUBLANES):
                sr = hst_ref[k, r0:r0 + SUBLANES, 0:LANES]
                si = hst_ref[k, r0:r0 + SUBLANES, LANES:2 * LANES]
                for t in range(tc):
                    r = t * bb + r0
                    xr = s_ref[k, r:r + SUBLANES, 0:LANES]
                    xi = s_ref[k, r:r + SUBLANES, LANES:2 * LANES]
                    sr, si = ar * sr - ai * si + xr, ar * si + ai * sr + xi
                    s_ref[k, r:r + SUBLANES, 0:LANES] = sr
                    s_ref[k, r:r + SUBLANES, LANES:2 * LANES] = si
                hst_ref[k, r0:r0 + SUBLANES, 0:LANES] = sr
                hst_ref[k, r0:r0 + SUBLANES, LANES:2 * LANES] = si
            part = jnp.dot(s_ref[k].astype(BF16), cw_ref[k], preferred_element_type=F32)
            acc = part if acc is None else acc + part
            if (k + 1) % slabs_per_out == 0:
                pieces.append(acc)
                acc = None
    hfin_ref[...] = hst_ref[...]
    y = jnp.concatenate(pieces, axis=-1) + d_ref[...] * u
    hg_tb = jax.nn.gelu(y).astype(BF16)
    hg = jnp.dot(permt_ref[...], hg_tb, preferred_element_type=F32).astype(BF16)
    z = jnp.dot(hg, wglu_ref[...], preferred_element_type=F32)
    mix = z[:, :d] * jax.nn.sigmoid(z[:, d:])
    o_ref[...] = (x + _rms(mix, gpost_ref[...])).reshape(bb, tc, d)


def _s5(x, h0, gpre, gpost, bblk, cw, a_re, a_im, dvec, wglu, dm, *, tc, bb):
    b, t, d = x.shape
    n_slab = cw.shape[0]
    n_ktile = bblk.shape[0]
    rows = tc * bb
    out_lanes_per_slab = (LANES // dm.ssm_state) * dm.ssm_group
    body = functools.partial(_s5_body, tc=tc, bb=bb, n_ktile=n_ktile, slabs_per_ktile=n_slab // n_ktile,
                             slabs_per_out=LANES // out_lanes_per_slab)
    r_out = jnp.arange(rows)
    perm = jax.nn.one_hot((r_out % bb) * tc + r_out // bb, rows, dtype=BF16)
    return pl.pallas_call(
        body,
        grid=(b // bb, t // tc),
        in_specs=[pl.BlockSpec((bb, tc, d), lambda bi, ti: (bi, ti, 0)),
                  pl.BlockSpec((n_slab, bb, 2 * LANES), lambda bi, ti: (0, bi, 0)),
                  _const_spec((rows, rows)), _const_spec((rows, rows)),
                  _const_spec((1, d)), _const_spec((1, d)),
                  _const_spec(bblk.shape), _const_spec(cw.shape),
                  _const_spec(a_re.shape), _const_spec(a_im.shape),
                  _const_spec((1, d)), _const_spec(wglu.shape)],
        out_specs=[pl.BlockSpec((bb, tc, d), lambda bi, ti: (bi, ti, 0)),
                   pl.BlockSpec((n_slab, bb, 2 * LANES), lambda bi, ti: (0, bi, 0))],
        out_shape=[jax.ShapeDtypeStruct((b, t, d), F32),
                   jax.ShapeDtypeStruct((n_slab, b, 2 * LANES), F32)],
        scratch_shapes=[pltpu.VMEM((n_slab, rows, 2 * LANES), F32),
                        pltpu.VMEM((n_slab, bb, 2 * LANES), F32)],
        compiler_params=_params("arbitrary", "arbitrary"),
        name="s5_mixer",
    )(x, h0, perm, perm.T, gpre, gpost, bblk, cw, a_re, a_im, dvec, wglu)


def _s5_weights(bb_re, bb_im, c_re, c_im, ab_re, ab_im, dm):
    g, p, i = bb_re.shape
    gps = LANES // p
    n_slab = g // gps
    g_tile = MXU_DIM // i
    n_ktile = g // g_tile
    eye_t = jnp.eye(g_tile, dtype=F32)

    def b_tiles(bm):
        bt = bm.transpose(0, 2, 1).reshape(n_ktile, g_tile, i, p)
        return jnp.einsum('jgip,gh->jgihp', bt, eye_t).reshape(n_ktile, g_tile * i, g_tile // gps, LANES)

    bblk = jnp.stack([b_tiles(bb_re), b_tiles(bb_im)], axis=3)
    bblk = bblk.reshape(n_ktile, g_tile * i, g_tile * p * 2).astype(BF16)

    spo = LANES // (gps * i)
    eye_g = jnp.eye(gps, dtype=F32)
    sel = jax.nn.one_hot(jnp.arange(n_slab) % spo, spo, dtype=F32)

    def c_slabs(cm):
        ct = cm.reshape(n_slab, gps, i, p)
        base = jnp.einsum('khip,hg->khpgi', ct, eye_g).reshape(n_slab, gps * p, gps * i)
        return jnp.einsum('krc,kq->krqc', base, sel).reshape(n_slab, LANES, LANES)

    cw = jnp.concatenate([c_slabs(c_re), -c_slabs(c_im)], axis=1).astype(BF16)
    a_re = ab_re.reshape(n_slab, 1, LANES)
    a_im = ab_im.reshape(n_slab, 1, LANES)
    return bblk, cw, a_re, a_im


def _state_to_slabs(h_re, h_im):
    b = h_re.shape[0]
    re = h_re.reshape(b, -1, LANES)
    im = h_im.reshape(b, -1, LANES)
    return jnp.concatenate([re, im], axis=-1).transpose(1, 0, 2)


def _slabs_to_state(hs, g, p):
    n_slab, b, _ = hs.shape
    hs = hs.transpose(1, 0, 2)
    re = hs[:, :, :LANES].reshape(b, g, p)
    im = hs[:, :, LANES:].reshape(b, g, p)
    return re, im


def _q_scale(dm):
    return (dm.qk_nope + dm.qk_rope) ** -0.5 * math.log2(math.e)


def _rope_block(blk, t1, t2):
    return blk * t1 + pltpu.roll(blk, 3 * LANES // 4, 1) * t2


def _mla_proj_body(x_ref, gpre_ref, win_ref, qn_ref, kvn_ref, wuq_ref, wuk_ref, wuv_ref, vone_ref, t1_ref, t2_ref,
                   q_ref, ckv_ref, kpe_ref, *kv_refs, q_lora, kv_lora, n_heads, qk_rope, q_scale):
    x = x_ref[...]
    h = _rms(x, gpre_ref[...]).astype(BF16)
    hin = jnp.dot(h, win_ref[...], preferred_element_type=F32)
    cq = _rms(hin[:, :q_lora], qn_ref[...]).astype(BF16)
    ckv = _rms(hin[:, q_lora:q_lora + kv_lora], kvn_ref[...])
    t1 = t1_ref[...]
    t2 = t2_ref[...]
    kpe_blk = _rope_block(hin[:, q_lora + kv_lora:], t1, t2)
    ckv_ref[...] = ckv
    kpe_ref[...] = pltpu.roll(kpe_blk, LANES // 2, 1)[:, :qk_rope]
    q = jnp.dot(cq, wuq_ref[...], preferred_element_type=F32)
    for hd in range(n_heads):
        sl = slice(hd * LANES, (hd + 1) * LANES)
        q_ref[:, sl] = (_rope_block(q[:, sl], t1, t2) * q_scale).astype(BF16)
    if kv_refs:
        k_ref, v_ref = kv_refs
        cb = ckv.astype(BF16)
        kn = jnp.dot(cb, wuk_ref[...], preferred_element_type=F32)
        for hd in range(n_heads):
            sl = slice(hd * LANES, (hd + 1) * LANES)
            k_ref[:, sl] = (kn[:, sl] + kpe_blk).astype(BF16)
        vt = lax.dot_general(wuv_ref[...], cb, (((1,), (1,)), ((), ())), preferred_element_type=F32)
        v_ref[...] = (vt + vone_ref[...]).astype(BF16)


def _mla_proj(x, gpre, win, qn, kvn, wuq, wuk, wuv, vone, t1, t2, dm, *, with_kv):
    n, d = x.shape
    tm = min(dm.tm, n)
    n_tab = t1.shape[0] // tm
    hq = dm.n_heads * LANES
    hv = wuv.shape[0]
    body = functools.partial(_mla_proj_body, q_lora=dm.q_lora, kv_lora=dm.kv_lora, n_heads=dm.n_heads,
                             qk_rope=dm.qk_rope, q_scale=_q_scale(dm))
    row = lambda w: pl.BlockSpec((tm, w), lambda i: (i, 0))
    tab = pl.BlockSpec((tm, LANES), lambda i: (i % n_tab, 0))
    out_specs = [row(hq), row(dm.kv_lora), row(dm.qk_rope)]
    out_shape = [jax.ShapeDtypeStruct((n, hq), BF16), jax.ShapeDtypeStruct((n, dm.kv_lora), F32),
                 jax.ShapeDtypeStruct((n, dm.qk_rope), F32)]
    if with_kv:
        out_specs += [row(hq), pl.BlockSpec((None, hv, tm), lambda i: (i, 0, 0))]
        out_shape += [jax.ShapeDtypeStruct((n, hq), BF16), jax.ShapeDtypeStruct((n // tm, hv, tm), BF16)]
    return pl.pallas_call(
        body,
        grid=(n // tm,),
        in_specs=[row(d), _const_spec((1, d)), _const_spec(win.shape), _const_spec(qn.shape),
                  _const_spec(kvn.shape), _const_spec(wuq.shape), _const_spec(wuk.shape),
                  _const_spec(wuv.shape), _const_spec(vone.shape), tab, tab],
        out_specs=out_specs,
        out_shape=out_shape,
        compiler_params=_params("arbitrary"),
        name="mla_proj_kv" if with_kv else "mla_proj",
    )(x, gpre, win, qn, kvn, wuq, wuk, wuv, vone, t1, t2)


def _rope_tables(pos, dm):
    half = dm.qk_rope // 2
    inv = ROPE_THETA ** (-jnp.arange(0, dm.qk_rope, 2, dtype=F32) / dm.qk_rope)
    ang = pos.astype(F32)[:, None] * inv[None, :]
    cos, sin = jnp.cos(ang), jnp.sin(ang)
    n = pos.shape[0]
    pad = LANES - dm.qk_nope - dm.qk_rope
    t1 = jnp.concatenate([jnp.ones((n, dm.qk_nope), F32), cos, cos, jnp.zeros((n, pad), F32)], axis=1)
    t2 = jnp.concatenate([jnp.zeros((n, dm.qk_nope), F32), -sin, sin, jnp.zeros((n, pad), F32)], axis=1)
    del half
    return t1, t2


def _mla_weights(w_in, w_uq, w_ukv, w_o, dm):
    hn, nope, rope, vh = dm.n_heads, dm.qk_nope, dm.qk_rope, dm.v_head
    half = rope // 2
    d = w_in.shape[0]
    lat = dm.q_lora + dm.kv_lora
    kpe = w_in[:, lat:]
    kpe_sw = jnp.concatenate([kpe[:, half:], kpe[:, :half]], axis=1)
    win = jnp.concatenate([w_in[:, :lat], jnp.zeros((d, nope), F32), kpe, kpe_sw], axis=1).astype(BF16)
    uq = w_uq.reshape(dm.q_lora, hn, nope + rope)
    wuq = jnp.concatenate([uq, uq[:, :, nope + half:], uq[:, :, nope:nope + half]], axis=2)
    wuq = wuq.reshape(dm.q_lora, hn * LANES).astype(BF16)
    ukv = w_ukv.reshape(dm.kv_lora, hn, nope + vh)
    wuk = jnp.concatenate([ukv[:, :, :nope], jnp.zeros((dm.kv_lora, hn, LANES - nope), F32)], axis=2)
    wuk = wuk.reshape(dm.kv_lora, hn * LANES).astype(BF16)
    wuv = jnp.concatenate([ukv[:, :, nope:].transpose(1, 2, 0), jnp.zeros((hn, VT_ROWS - vh, dm.kv_lora), F32)],
                          axis=1).reshape(hn * VT_ROWS, dm.kv_lora).astype(BF16)
    vone = jnp.tile((jnp.arange(VT_ROWS) == vh).astype(F32), hn)[:, None]
    wuk_t = jnp.concatenate([ukv[:, :, :nope].transpose(1, 2, 0),
                             jnp.zeros((hn, LANES - nope, dm.kv_lora), F32)], axis=1).astype(BF16)
    wuv_p = jnp.concatenate([ukv[:, :, nope:].transpose(1, 0, 2),
                             jnp.zeros((hn, dm.kv_lora, LANES - vh), F32)], axis=2).astype(BF16)
    wo_p = jnp.concatenate([w_o.reshape(hn, vh, d), jnp.zeros((hn, LANES - vh, d), F32)], axis=1).astype(BF16)
    return dict(win=win, wuq=wuq, wuk=wuk, wuv=wuv, vone=vone, wuk_t=wuk_t, wuv_p=wuv_p, wo_p=wo_p, wo=w_o.astype(BF16))


def _attn_prompt_body(q_ref, k_ref, vt_ref, o_ref, sa_ref, sb_ref, m_ref, acc_ref, *, tq, v_head):
    qi = pl.program_id(2)
    n_blocks = pl.num_programs(2)
    n_hh = q_ref.shape[-1] // LANES
    key_minus_qry = (lax.broadcasted_iota(jnp.int32, (tq, tq), 0)
                     - lax.broadcasted_iota(jnp.int32, (tq, tq), 1))
    nt = (((1,), (1,)), ((), ()))
    qh = [q_ref[:, hh * LANES:(hh + 1) * LANES] for hh in range(n_hh)]

    def scores_into(dst_ref, ki):
        r = pl.multiple_of(ki * tq, tq)
        for hh in range(n_hh):
            kk = k_ref[pl.ds(r, tq), hh * LANES:(hh + 1) * LANES]
            dst_ref[hh] = lax.dot_general(kk, qh[hh], nt, preferred_element_type=F32)

    def accumulate(src_ref, ki, masked):
        for hh in range(n_hh):
            st = src_ref[hh]
            if masked:
                st = jnp.where(key_minus_qry <= (qi - ki) * tq, st, -jnp.inf)
            m = m_ref[hh]
            m_new = jnp.maximum(m, jnp.max(st, axis=0, keepdims=True))
            alpha = jnp.exp2(m - m_new)
            pt = jnp.exp2(st - m_new).astype(BF16)
            m_ref[hh] = m_new
            vt = vt_ref[ki, hh * VT_ROWS:(hh + 1) * VT_ROWS, :]
            acc_ref[hh] = alpha * acc_ref[hh] + jnp.dot(vt, pt, preferred_element_type=F32)

    m_ref[...] = jnp.full(m_ref.shape, -jnp.inf, F32)
    acc_ref[...] = jnp.zeros(acc_ref.shape, F32)

    scores_into(sa_ref, 0)

    def pair(kp, carry):
        scores_into(sb_ref, 2 * kp + 1)
        accumulate(sa_ref, 2 * kp, False)
        scores_into(sa_ref, 2 * kp + 2)
        accumulate(sb_ref, 2 * kp + 1, False)
        return carry

    lax.fori_loop(0, qi // 2, pair, 0)
    k0 = 2 * (qi // 2)
    scores_into(sb_ref, jnp.minimum(k0 + 1, n_blocks - 1))
    accumulate(sa_ref, k0, True)

    @pl.when(qi % 2 == 1)
    def _():
        accumulate(sb_ref, k0 + 1, True)

    ot = jnp.concatenate([acc_ref[hh, 0:v_head, :] / acc_ref[hh, v_head:v_head + 1, :] for hh in range(n_hh)],
                         axis=0)
    o_ref[...] = ot.T.astype(o_ref.dtype)


def _attn_prompt(q, k, vt, batch, seq, dm):
    tq = vt.shape[-1]
    nq = seq // tq
    nh = dm.attn_heads
    k3 = k.reshape(batch, seq, -1)
    vt4 = vt.reshape(batch, nq, vt.shape[1], tq)
    body = functools.partial(_attn_prompt_body, tq=tq, v_head=dm.v_head)
    return pl.pallas_call(
        body,
        grid=(batch, dm.n_heads // nh, nq),
        in_specs=[pl.BlockSpec((tq, nh * LANES), lambda b, h, i: (b * nq + i, h)),
                  pl.BlockSpec((None, seq, nh * LANES), lambda b, h, i: (b, 0, h)),
                  pl.BlockSpec((None, nq, nh * VT_ROWS, tq), lambda b, h, i: (b, 0, h, 0))],
        out_specs=pl.BlockSpec((tq, nh * dm.v_head), lambda b, h, i: (b * nq + i, h)),
        out_shape=jax.ShapeDtypeStruct((batch * seq, dm.n_heads * dm.v_head), BF16),
        scratch_shapes=[pltpu.VMEM((nh, tq, tq), F32), pltpu.VMEM((nh, tq, tq), F32),
                        pltpu.VMEM((nh, 1, tq), F32), pltpu.VMEM((nh, VT_ROWS, tq), F32)],
        compiler_params=_params("arbitrary", "arbitrary", "arbitrary"),
        name="attn_prompt",
    )(q, k3, vt4)


def _sample_q_body(q_ref, wukt_ref, ql_ref, qr_ref, *, n_heads, qk_nope, seq):
    n = q_ref.shape[0]
    lane = lax.broadcasted_iota(jnp.int32, (LANES, LANES), 1)
    src = lax.broadcasted_iota(jnp.int32, (LANES, LANES), 0)
    sel = (src == lane + qk_nope).astype(BF16)
    for hd in range(n_heads):
        qh = q_ref[:, hd * LANES:(hd + 1) * LANES]
        ql = jnp.dot(qh, wukt_ref[hd], preferred_element_type=F32)
        qr = jnp.dot(qh, sel, preferred_element_type=F32)
        ql_ref[:, hd] = ql.reshape(n // seq, seq, ql.shape[-1])
        qr_ref[:, hd] = qr.reshape(n // seq, seq, LANES)


def _sample_q(q, wuk_t, dec_batch, dec_seq, dm):
    c = dm.kv_lora
    body = functools.partial(_sample_q_body, n_heads=dm.n_heads, qk_nope=dm.qk_nope, seq=dec_seq)
    return pl.pallas_call(
        body,
        out_shape=[jax.ShapeDtypeStruct((dec_batch, dm.n_heads, dec_seq, c), F32),
                   jax.ShapeDtypeStruct((dec_batch, dm.n_heads, dec_seq, LANES), F32)],
        compiler_params=_params(),
        name="sample_q",
    )(q, wuk_t)


def _sample_attn_body(pt_ref, ql_ref, qr_ref, ckvn_ref, kpent_ref, ckv_hbm, kpet_hbm, o_ref,
                      m_ref, l_ref, acc_ref, cat_ref, kp_ref, kn_ref, ckv_buf, kpe_buf, sem, *, n_pages, seq,
                      n_chunks):
    b = pl.program_id(0)
    j = pl.program_id(1)
    n_b = pl.num_programs(0)
    n_j = pl.num_programs(1)
    step = b * n_j + j
    slot = step % 2
    n_seq, rows, c = acc_ref.shape
    page = ckv_buf.shape[3]
    rope = kpe_buf.shape[3]

    def page_copies(bb, jj, sl, i, g):
        pg = pt_ref[bb * n_seq + i, jj * n_pages + g]
        return (pltpu.make_async_copy(ckv_hbm.at[pg], ckv_buf.at[sl, i, g], sem.at[sl, 0]),
                pltpu.make_async_copy(kpet_hbm.at[pg], kpe_buf.at[sl, i, g], sem.at[sl, 1]))

    def start_fetch(bb, jj, sl):
        for i in range(n_seq):
            for g in range(n_pages):
                for cp in page_copies(bb, jj, sl, i, g):
                    cp.start()

    def wait_fetch(bb, jj, sl):
        for i in range(n_seq):
            for g in range(n_pages):
                for cp in page_copies(bb, jj, sl, i, g):
                    cp.wait()

    @pl.when(step == 0)
    def _():
        start_fetch(0, 0, 0)

    last_j = j + 1 == n_j
    j_next = jnp.where(last_j, 0, j + 1)
    b_next = jnp.where(last_j, jnp.where(b + 1 == n_b, 0, b + 1), b)
    start_fetch(b_next, j_next, 1 - slot)

    ql =[ql_ref[i].reshape(rows, c).astype(BF16) for i in range(n_seq)]
    qr = [qr_ref[i].reshape(rows, LANES).astype(BF16) for i in range(n_seq)]
    nt = (((1,), (1,)), ((), ()))

    @pl.when(j == 0)
    def _():
        kp_ref[...] = jnp.zeros_like(kp_ref)
        kn_ref[...] = jnp.zeros_like(kn_ref)
        for i in range(n_seq):
            kn_ref[i, 0:rope, 0:seq] = kpent_ref[i]
            cn = jnp.concatenate([ckvn_ref[i], jnp.zeros((page - seq, c), F32)], axis=0).astype(BF16)
            s = (lax.dot_general(ql[i], cn, nt, preferred_element_type=F32)
                 + jnp.dot(qr[i], kn_ref[i].astype(BF16), preferred_element_type=F32))
            r = lax.broadcasted_iota(jnp.int32, s.shape, 0)
            k = lax.broadcasted_iota(jnp.int32, s.shape, 1)
            s = jnp.where(k <= r % seq, s, -jnp.inf)
            m = jnp.max(s, axis=-1, keepdims=True)
            p = jnp.exp2(s - m)
            m_ref[i] = m
            l_ref[i] = jnp.sum(p, axis=-1, keepdims=True)
            acc_ref[i] = jnp.dot(p.astype(BF16), cn, preferred_element_type=F32)

    wait_fetch(b, j, slot)
    cp = n_pages // n_chunks * page
    for i in range(n_seq):
        s = []
        for ch in range(n_chunks):
            for g in range(ch * n_pages // n_chunks, (ch + 1) * n_pages // n_chunks):
                cat_ref[i, g * page:(g + 1) * page, :] = ckv_buf[slot, i, g].astype(BF16)
                kp_ref[i, 0:rope, g * page:(g + 1) * page] = kpe_buf[slot, i, g].astype(BF16)
            s.append(lax.dot_general(ql[i], cat_ref[i, ch * cp:(ch + 1) * cp, :], nt, preferred_element_type=F32)
                     + jnp.dot(qr[i], kp_ref[i, :, ch * cp:(ch + 1) * cp], preferred_element_type=F32))
        m = m_ref[i]
        m_new = m
        for sc in s:
            m_new = jnp.maximum(m_new, jnp.max(sc, axis=-1, keepdims=True))
        alpha = jnp.exp2(m - m_new)
        l = alpha * l_ref[i]
        acc = alpha * acc_ref[i]
        for ch, sc in enumerate(s):
            p = jnp.exp2(sc - m_new)
            l = l + jnp.sum(p, axis=-1, keepdims=True)
            acc = acc + jnp.dot(p.astype(BF16), cat_ref[i, ch * cp:(ch + 1) * cp, :], preferred_element_type=F32)
        m_ref[i] = m_new
        l_ref[i] = l
        acc_ref[i] = acc

    @pl.when(last_j)
    def _():
        o_ref[...] = acc_ref[...] / l_ref[...]

    @pl.when(step == n_b * n_j - 1)
    def _():
        wait_fetch(b_next, j_next, 1 - slot)


def _sample_attn(q_lat, q_rope, ckv_new, kpe_new_t, cache_ckv, cache_kpe_t, page_table, dm):
    bd, hn, seq, c = q_lat.shape
    n_log = page_table.shape[1]
    pps = min(dm.pages_per_step, n_log)
    ns = dm.seqs_per_step
    page = cache_ckv.shape[1]
    rope = cache_kpe_t.shape[1]
    rows = hn * seq
    n_chunks = math.gcd(dm.page_chunks, pps)
    body = functools.partial(_sample_attn_body, n_pages=pps, seq=seq, n_chunks=n_chunks)
    grid_spec = pltpu.PrefetchScalarGridSpec(
        num_scalar_prefetch=1,
        grid=(bd // ns, n_log // pps),
        in_specs=[pl.BlockSpec((ns, hn, seq, c), lambda b, j, pt: (b, 0, 0, 0)),
                  pl.BlockSpec((ns, hn, seq, LANES), lambda b, j, pt: (b, 0, 0, 0)),
                  pl.BlockSpec((ns, seq, c), lambda b, j, pt: (b, 0, 0)),
                  pl.BlockSpec((ns, rope, seq), lambda b, j, pt: (b, 0, 0)),
                  pl.BlockSpec(memory_space=pl.ANY), pl.BlockSpec(memory_space=pl.ANY)],
        out_specs=pl.BlockSpec((ns, rows, c), lambda b, j, pt: (b, 0, 0)),
        scratch_shapes=[pltpu.VMEM((ns, rows, 1), F32), pltpu.VMEM((ns, rows, 1), F32), pltpu.VMEM((ns, rows, c), F32),
                        pltpu.VMEM((ns, pps * page, c), BF16), pltpu.VMEM((ns, LANES, pps * page), BF16),
                        pltpu.VMEM((ns, LANES, page), F32),
                        pltpu.VMEM((2, ns, pps, page, c), F32), pltpu.VMEM((2, ns, pps, rope, page), F32),
                        pltpu.SemaphoreType.DMA((2, 2))],
    )
    return pl.pallas_call(
        body,
        grid_spec=grid_spec,
        out_shape=jax.ShapeDtypeStruct((bd, rows, c), F32),
        compiler_params=_params("arbitrary", "arbitrary"),
        name="attn_sample",
    )(page_table, q_lat, q_rope, ckv_new, kpe_new_t, cache_ckv, cache_kpe_t)


def _sample_out_body(x_ref, ol_ref, wuv_ref, wo_ref, gpost_ref, y_ref, *, n_heads):
    bb, _, seq, c = ol_ref.shape
    h = None
    for hd in range(n_heads):
        ol = ol_ref[:, hd].reshape(bb * seq, c).astype(BF16)
        oh = jnp.dot(ol, wuv_ref[hd], preferred_element_type=F32).astype(BF16)
        part = jnp.dot(oh, wo_ref[hd], preferred_element_type=F32)
        h = part if h is None else h + part
    y_ref[...] = x_ref[...] + _rms(h, gpost_ref[...])


def _sample_out(x, o_lat, wuv_p, wo_p, gpost, dm):
    n, d = x.shape
    bd, hn, seq, c = o_lat.shape
    bb = min(dm.out_bb, bd)
    body = functools.partial(_sample_out_body, n_heads=hn)
    return pl.pallas_call(
        body,
        grid=(bd // bb,),
        in_specs=[pl.BlockSpec((bb * seq, d), lambda i: (i, 0)),
                  pl.BlockSpec((bb, hn, seq, c), lambda i: (i, 0, 0, 0)),
                  _const_spec(wuv_p.shape), _const_spec(wo_p.shape), _const_spec((1, d))],
        out_specs=pl.BlockSpec((bb * seq, d), lambda i: (i, 0)),
        out_shape=jax.ShapeDtypeStruct((n, d), F32),
        compiler_params=_params("arbitrary"),
        name="sample_out",
    )(x, o_lat, wuv_p, wo_p, gpost)


def _forward(x, pos_tab, h0_re, h0_im, attend, w, dm, *, s5_tc, s5_bb):
    b, t, d = x.shape
    n = b * t
    xf = x.reshape(n, d)
    g = lambda a, i, k: a[i, k][None, :]
    xf = _ffn(xf, g(w['norm_pre'], 0, 0), g(w['norm_post'], 0, 0), *w['ffn'], 0, 0, dm)
    y3, hfin = _s5(xf.reshape(b, t, d), _state_to_slabs(h0_re, h0_im), g(w['norm_pre'], 0, 1),
                   g(w['norm_post'], 0, 1), w['bblk'], w['cw'], w['a_re'], w['a_im'], w['ssm_d'], w['wglu'], dm,
                   tc=s5_tc, bb=s5_bb)
    xf = y3.reshape(n, d)
    s_re, s_im = _slabs_to_state(hfin, d // dm.ssm_group, dm.ssm_state)
    xf = _ffn(xf, g(w['norm_pre'], 0, 2), g(w['norm_post'], 0, 2), *w['ffn'], 0, 1, dm)
    xf = _ffn(xf, g(w['norm_pre'], 1, 0), g(w['norm_post'], 1, 0), *w['ffn'], 1, 0, dm)
    xf, attn, c_kv, k_pe = attend(xf, pos_tab, g(w['norm_pre'], 1, 1), g(w['norm_post'], 1, 1))
    xf = _ffn(xf, g(w['norm_pre'], 1, 2), g(w['norm_post'], 1, 2), *w['ffn'], 1, 1, dm, attn=attn)
    return (xf.reshape(b, t, d), s_re[None], s_im[None],
            c_kv.reshape(1, b, t, -1), k_pe.reshape(1, b, t, -1))


def _run(dm, x_prompt, x_sample, state_ssm_re, state_ssm_im, cache_kv_latent, cache_k_rope, page_table,
         norm_pre, norm_post, ffn_w_gate, ffn_w_up, ffn_w_down,
         ssm_a_re, ssm_a_im, ssm_log_dt, ssm_b_re, ssm_b_im, ssm_c_re, ssm_c_im, ssm_d, ssm_w_glu,
         mla_w_in, mla_q_norm, mla_kv_norm, mla_w_uq, mla_w_ukv, mla_w_o):
    b_p, s_p, d = x_prompt.shape
    b_s, s_s, _ = x_sample.shape
    past_len = page_table.shape[1] * dm.page_size
    depth = norm_pre.shape[0]
    assert depth == 2 and ssm_a_re.shape[0] == 1 and mla_w_in.shape[0] == 1

    ab_re, ab_im, bb_re, bb_im = _ssm_prep(ssm_a_re[0], ssm_a_im[0], ssm_log_dt[0], ssm_b_re[0], ssm_b_im[0])
    bblk, cw, a_re, a_im = _s5_weights(bb_re, bb_im, ssm_c_re[0], ssm_c_im[0], ab_re, ab_im, dm)
    mw = _mla_weights(mla_w_in[0], mla_w_uq[0], mla_w_ukv[0], mla_w_o[0], dm)
    w = dict(norm_pre=norm_pre, norm_post=norm_post,
             ffn=(ffn_w_gate.astype(BF16), ffn_w_up.astype(BF16), ffn_w_down.astype(BF16)),
             bblk=bblk, cw=cw, a_re=a_re, a_im=a_im, ssm_d=ssm_d[0][None, :], wglu=ssm_w_glu[0].astype(BF16))
    qn = mla_q_norm[0][None, :]
    kvn = mla_kv_norm[0][None, :]

    def prompt_attend(xf, tabs, gpre, gpost):
        q, c_kv, k_pe, kmat, vt = _mla_proj(xf, gpre, mw['win'], qn, kvn, mw['wuq'], mw['wuk'], mw['wuv'],
                                            mw['vone'], *tabs, dm, with_kv=True)
        o = _attn_prompt(q, kmat, vt, b_p, s_p, dm)
        return xf, (o, mw['wo'], gpost), c_kv, k_pe

    def sample_attend(xf, tabs, gpre, gpost):
        q, c_kv, k_pe = _mla_proj(xf, gpre, mw['win'], qn, kvn, mw['wuq'], mw['wuk'], mw['wuv'],
                                  mw['vone'], *tabs, dm, with_kv=False)
        q_lat, q_rope = _sample_q(q, mw['wuk_t'], b_s, s_s, dm)
        o_lat = _sample_attn(q_lat, q_rope, c_kv.reshape(b_s, s_s, -1), k_pe.reshape(b_s, s_s, -1).swapaxes(1, 2),
                             cache_kv_latent[0], cache_k_rope[0].swapaxes(1, 2), page_table, dm)
        o_lat = o_lat.reshape(b_s, dm.n_heads, s_s, -1)
        return _sample_out(xf, o_lat, mw['wuv_p'], mw['wo_p'], gpost, dm), None, c_kv, k_pe

    tm_p = min(dm.tm, b_p * s_p)
    tabs_p = _rope_tables(jnp.arange(max(s_p, tm_p), dtype=jnp.int32) % s_p, dm)
    h0 = jnp.zeros((b_p,) + state_ssm_re.shape[2:], F32)
    tc_p = max(dm.s5_rows // b_p, 1)
    out_p = _forward(x_prompt, tabs_p, h0, h0, prompt_attend, w, dm, s5_tc=min(tc_p, s_p), s5_bb=b_p)
    tm_s = min(dm.tm, b_s * s_s)
    tabs_s = _rope_tables(past_len + jnp.arange(max(s_s, tm_s), dtype=jnp.int32) % s_s, dm)
    bb_s = min(max(dm.s5_rows // s_s, SUBLANES), b_s)
    out_s = _forward(x_sample, tabs_s, state_ssm_re[0], state_ssm_im[0], sample_attend, w, dm,
                     s5_tc=s_s, s5_bb=bb_s)
    y_p, re_p, im_p, lat_p, kpe_p = out_p
    y_s, re_s, im_s, lat_s, kpe_s = out_s
    return (y_p, y_s, re_p, im_p, re_s, im_s, lat_p, kpe_p, lat_s, kpe_s)


def kernel(x_prompt, x_sample, state_ssm_re, state_ssm_im, cache_kv_latent, cache_k_rope, page_table,
           norm_pre, norm_post, ffn_w_gate, ffn_w_up, ffn_w_down,
           ssm_a_re, ssm_a_im, ssm_log_dt, ssm_b_re, ssm_b_im, ssm_c_re, ssm_c_im, ssm_d, ssm_w_glu,
           mla_w_in, mla_q_norm, mla_kv_norm, mla_w_uq, mla_w_ukv, mla_w_o):
    return _run(DIMS, x_prompt, x_sample, state_ssm_re, state_ssm_im, cache_kv_latent, cache_k_rope, page_table,
                norm_pre, norm_post, ffn_w_gate, ffn_w_up, ffn_w_down,
                ssm_a_re, ssm_a_im, ssm_log_dt, ssm_b_re, ssm_b_im, ssm_c_re, ssm_c_im, ssm_d, ssm_w_glu,
                mla_w_in, mla_q_norm, mla_kv_norm, mla_w_uq, mla_w_ukv, mla_w_o)
```
